```python
import math
import jax, jax.numpy as jnp
from jax import lax
import numpy as np

D_MODEL = 2048
BATCH = 4
SEQ = 8192
DEPTH = 1

CHUNK = 64
GLA_HEADS = 4
GLA_DK = D_MODEL // 2 // GLA_HEADS
GLA_DV = D_MODEL // GLA_HEADS
GLA_RANK = 16
GLA_TAU = 16.0
GMLP_BLOCK = 128
GMLP_GROUPS = 4
GMLP_DG = D_MODEL // GMLP_GROUPS
D_FF = ((8 * D_MODEL + 3 * 256 - 1) // (3 * 256)) * 256
PLE_DIM = 256
ALPHA = (2 * DEPTH) ** 0.25
BETA = (8 * DEPTH) ** -0.25
LN_EPS = 1e-5

SPLIT_SIZES = (
    GLA_HEADS * GLA_DK,
    GLA_HEADS * GLA_DK,
    GLA_HEADS * GLA_DV,
    GLA_HEADS * GLA_DV,
    GLA_HEADS * GLA_RANK,
    D_MODEL,
    D_MODEL,
    D_MODEL,
    D_MODEL,
)
D_IN = sum(SPLIT_SIZES)

kernel_name = "hybrid_gla_gmlp_deepnorm_block"


def _split_points():
    pts, acc = [], 0
    for s in SPLIT_SIZES[:-1]:
        acc += s
        pts.append(acc)
    return pts


def layer_norm(x, g, b):
    xf = x.astype(jnp.float32)
    mu = jnp.mean(xf, axis=-1, keepdims=True)
    var = jnp.mean(jnp.square(xf - mu), axis=-1, keepdims=True)
    return ((xf - mu) * lax.rsqrt(var + LN_EPS) * g + b).astype(x.dtype)


def rms_norm(x, g):
    xf = x.astype(jnp.float32)
    return (xf * lax.rsqrt(jnp.mean(jnp.square(xf), axis=-1, keepdims=True) + LN_EPS) * g).astype(x.dtype)


def gla_chunk_causal(q, k, v, log_a):
    bsz, s, h, dk = q.shape
    dv = v.shape[-1]
    nc = s // CHUNK
    f32 = jnp.float32
    q = q.astype(f32).reshape(bsz, nc, CHUNK, h, dk) * (dk ** -0.5)
    k = k.astype(f32).reshape(bsz, nc, CHUNK, h, dk)
    v = v.astype(f32).reshape(bsz, nc, CHUNK, h, dv)
    b = jnp.cumsum(log_a.astype(f32).reshape(bsz, nc, CHUNK, h, dk), axis=2)
    b_end = b[:, :, -1:]
    k_t = k * jnp.exp(b_end - b)
    q_t = q * jnp.exp(b_end)
    scores = jnp.einsum('bnihd,bnjhd->bnhij', q, k_t)
    o_intra = jnp.einsum('bnhij,bnjhv->bnihv', scores, v)

    def step(state, xs):
        qc, kc, vc, dc = xs
        o = jnp.einsum('blhd,bhdv->blhv', qc, state)
        state = jnp.exp(dc)[..., None] * state + jnp.einsum('blhd,blhv->bhdv', kc, vc)
        return state, o

    xs = (jnp.moveaxis(q_t, 1, 0), jnp.moveaxis(k_t, 1, 0), jnp.moveaxis(v, 1, 0),
          jnp.moveaxis(b_end[:, :, 0], 1, 0))
    state0 = jnp.zeros((bsz, h, dk, dv), f32)
    _, o_inter = lax.scan(step, state0, xs)
    o = o_intra + jnp.moveaxis(o_inter, 0, 1)
    return o.reshape(bsz, s, h, dv)


def gmlp_spatial_gate(u, z, ln_g, ln_b, w_s, b_s):
    bsz, s, d = u.shape
    nb = s // GMLP_BLOCK
    zn = layer_norm(z.reshape(bsz, nb, GMLP_BLOCK, GMLP_GROUPS, GMLP_DG), ln_g, ln_b)
    pos_chunk = jnp.arange(GMLP_BLOCK) // CHUNK
    mask = pos_chunk[:, None] >= pos_chunk[None, :]
    w = jnp.where(mask[None], w_s, jnp.zeros_like(w_s))
    sg = jnp.einsum('gij,bnjgc->bnigc', w, zn) + b_s.T[None, None, :, :, None]
    return u * sg.reshape(bsz, s, d)


def setup_inputs(seed: int = 0) -> dict:
    key = jax.random.key(seed)
    ks = jax.random.split(key, 24)
    f32 = jnp.float32
    nrm = lambda k, shape: jax.random.normal(k, shape, f32)
    L = DEPTH
    return {
        "x": nrm(ks[0], (BATCH, SEQ, D_MODEL)),
        "p": nrm(ks[1], (DEPTH, BATCH, SEQ, PLE_DIM)),
        "ln0_g": 1.0 + 0.01 * nrm(ks[2], (D_MODEL,)),
        "ln0_b": 0.01 * nrm(ks[3], (D_MODEL,)),
        "w_in": nrm(ks[4], (L, D_MODEL, D_IN)) * D_MODEL ** -0.5,
        "b_in": 0.01 * nrm(ks[5], (L, D_IN)),
        "w_f2": nrm(ks[6], (L, GLA_HEADS, GLA_RANK, GLA_DK)) * GLA_RANK ** -0.5,
        "b_f2": 0.1 * nrm(ks[7], (L, GLA_HEADS, GLA_DK)),
        "gla_norm_g": 1.0 + 0.01 * nrm(ks[8], (L, GLA_HEADS, GLA_DV)),
        "gmlp_ln_g": 1.0 + 0.01 * nrm(ks[9], (L, GMLP_GROUPS, GMLP_DG)),
        "gmlp_ln_b": 0.01 * nrm(ks[10], (L, GMLP_GROUPS, GMLP_DG)),
        "w_s": nrm(ks[11], (L, GMLP_GROUPS, GMLP_BLOCK, GMLP_BLOCK)) * GMLP_BLOCK ** -0.5,
        "b_s": 1.0 + 0.01 * nrm(ks[12], (L, GMLP_GROUPS, GMLP_BLOCK)),
        "w_o": nrm(ks[13], (L, D_MODEL, D_MODEL)) * (D_MODEL ** -0.5 * BETA),
        "ln1_g": 1.0 + 0.01 * nrm(ks[14], (L, D_MODEL)),
        "ln1_b": 0.01 * nrm(ks[15], (L, D_MODEL)),
        "w_gu": nrm(ks[16], (L, D_MODEL, 2 * D_FF)) * D_MODEL ** -0.5,
        "w_down": nrm(ks[17], (L, D_FF, D_MODEL)) * (D_FF ** -0.5 * BETA),
        "w_pg": nrm(ks[18], (L, D_MODEL, D_MODEL)) * D_MODEL ** -0.5,
        "b_pg": 0.01 * nrm(ks[19], (L, D_MODEL)),
        "w_pu": nrm(ks[20], (L, PLE_DIM, D_MODEL)) * (PLE_DIM ** -0.5 * BETA),
        "ln2_g": 1.0 + 0.01 * nrm(ks[21], (L, D_MODEL)),
        "ln2_b": 0.01 * nrm(ks[22], (L, D_MODEL)),
    }


def reference(x, p, ln0_g, ln0_b, w_in, b_in, w_f2, b_f2, gla_norm_g, gmlp_ln_g, gmlp_ln_b,
              w_s, b_s, w_o, ln1_g, ln1_b, w_gu, w_down, w_pg, b_pg, w_pu, ln2_g, ln2_b):
    bsz, s, d = x.shape
    pts = _split_points()
    h = layer_norm(x, ln0_g, ln0_b)
    for i in range(DEPTH):
        proj = h @ w_in[i] + b_in[i]
        q, k, v, og, fg, u, z, ga, gb = jnp.split(proj, pts, axis=-1)
        fg = fg.reshape(bsz, s, GLA_HEADS, GLA_RANK)
        f_logit = jnp.einsum('bshr,hrd->bshd', fg, w_f2[i]) + b_f2[i]
        log_a = jax.nn.log_sigmoid(f_logit.astype(jnp.float32)) / GLA_TAU
        o = gla_chunk_causal(q.reshape(bsz, s, GLA_HEADS, GLA_DK),
                             k.reshape(bsz, s, GLA_HEADS, GLA_DK),
                             v.reshape(bsz, s, GLA_HEADS, GLA_DV), log_a)
        y_a = rms_norm(o, gla_norm_g[i]).reshape(bsz, s, d) * jax.nn.silu(og)
        y_b = gmlp_spatial_gate(jax.nn.gelu(u, approximate=False), jax.nn.gelu(z, approximate=False),
                                gmlp_ln_g[i], gmlp_ln_b[i], w_s[i], b_s[i])
        m = jax.nn.sigmoid(ga) * y_a + jax.nn.sigmoid(gb) * y_b
        h = layer_norm(ALPHA * h + m @ w_o[i], ln1_g[i], ln1_b[i])
        gt, up = jnp.split(h @ w_gu[i], 2, axis=-1)
        ffn = (jax.nn.silu(gt) * up) @ w_down[i]
        ple = jax.nn.sigmoid(h @ w_pg[i] + b_pg[i]) * (p[i] @ w_pu[i])
        h = layer_norm(ALPHA * h + ffn + ple, ln2_g[i], ln2_b[i])
    return h
```

```python
import functools

import jax
import jax.numpy as jnp
from jax import lax
from jax.experimental import pallas as pl
from jax.experimental.pallas import tpu as pltpu

F32 = jnp.float32
BF16 = jnp.bfloat16

D_MODEL = 2048
CHUNK = 64
GLA_HEADS = 4
GLA_DK = 256
GLA_DV = 512
GLA_RANK = 16
GLA_TAU = 16.0
GMLP_BLOCK = 128
GMLP_GROUPS = 4
GMLP_DG = 512
D_FF = 5632
PLE_DIM = 256
ALPHA = 2.0 ** 0.25
LN_EPS = 1e-5
SQRT_HALF = 0.7071067811865476

N_QK = GLA_HEADS * GLA_DK
N_MAIN = 2 * N_QK + 6 * D_MODEL
FG_PAD = 128

VMEM_LIMIT = 56 * 1024 * 1024

TM1, TN1, RC1 = 1024, 1024, 256
TS2 = 256
TM3 = 256
TM4, TF4 = 512, 512


def _ln_rows(x, g, b):
    mu = jnp.mean(x, axis=-1, keepdims=True)
    xc = x - mu
    var = jnp.mean(xc * xc, axis=-1, keepdims=True)
    return xc * lax.rsqrt(var + LN_EPS) * g + b


def _gelu(x):
    return 0.5 * x * (1.0 + lax.erf(x * SQRT_HALF))


def _sigmoid(x):
    return 1.0 / (1.0 + jnp.exp(-x))


def _inproj_kernel(x_ref, g0_ref, b0_ref, w_ref, bias_ref, wfg_ref, bfg_ref,
                   zg_ref, zb_ref, out_ref, fg_ref, hb_ref):
    j = pl.program_id(1)

    @pl.when(j == 0)
    def _():
        def body(r, c):
            rows = pl.ds(pl.multiple_of(r * RC1, RC1), RC1)
            h = _ln_rows(x_ref[rows, :], g0_ref[...], b0_ref[...])
            hb = h.astype(BF16)
            hb_ref[rows, :] = hb
            fg = jnp.dot(hb, wfg_ref[...], preferred_element_type=F32) + bfg_ref[...]
            fg_ref[rows, :] = fg.astype(BF16)
            return c
        lax.fori_loop(0, TM1 // RC1, body, 0)

    def section(act):
        for r in range(TM1 // RC1):
            rows = slice(r * RC1, (r + 1) * RC1)
            acc = jnp.dot(hb_ref[rows, :], w_ref[...], preferred_element_type=F32)
            out_ref[rows, :] = act(acc + bias_ref[...]).astype(BF16)

    def z_act(a):
        halves = []
        for g in range(TN1 // GMLP_DG):
            cs = slice(g * GMLP_DG, (g + 1) * GMLP_DG)
            halves.append(_ln_rows(_gelu(a[:, cs]), zg_ref[:, cs], zb_ref[:, cs]))
        return jnp.concatenate(halves, axis=-1)

    pl.when(j < 4)(lambda: section(lambda a: a))
    pl.when((j >= 4) & (j < 6))(lambda: section(lambda a: a * _sigmoid(a)))
    pl.when((j >= 6) & (j < 8))(lambda: section(_gelu))
    pl.when((j >= 8) & (j < 10))(lambda: section(z_act))
    pl.when(j >= 10)(lambda: section(_sigmoid))


def _inproj(x2, g0, b0, w_main, b_main, w_fg, b_fg, zg, zb):
    m = x2.shape[0]
    grid = (m // TM1, N_MAIN // TN1)
    return pl.pallas_call(
        _inproj_kernel,
        grid=grid,
        in_specs=[
            pl.BlockSpec((TM1, D_MODEL), lambda i, j: (i, 0)),
            pl.BlockSpec((1, D_MODEL), lambda i, j: (0, 0)),
            pl.BlockSpec((1, D_MODEL), lambda i, j: (0, 0)),
            pl.BlockSpec((D_MODEL, TN1), lambda i, j: (0, j)),
            pl.BlockSpec((1, TN1), lambda i, j: (0, j)),
            pl.BlockSpec((D_MODEL, FG_PAD), lambda i, j: (0, 0)),
            pl.BlockSpec((1, FG_PAD), lambda i, j: (0, 0)),
            pl.BlockSpec((1, TN1), lambda i, j: (0, jnp.clip(j - 8, 0, 1))),
            pl.BlockSpec((1, TN1), lambda i, j: (0, jnp.clip(j - 8, 0, 1))),
        ],
        out_specs=[
            pl.BlockSpec((TM1, TN1), lambda i, j: (i, j)),
            pl.BlockSpec((TM1, FG_PAD), lambda i, j: (i, 0)),
        ],
        out_shape=[
            jax.ShapeDtypeStruct((m, N_MAIN), BF16),
            jax.ShapeDtypeStruct((m, FG_PAD), BF16),
        ],
        scratch_shapes=[pltpu.VMEM((TM1, D_MODEL), BF16)],
        compiler_params=pltpu.CompilerParams(
            dimension_semantics=("arbitrary", "arbitrary"),
            vmem_limit_bytes=VMEM_LIMIT),
        name="inproj",
    )(x2, g0, b0, w_main, b_main, w_fg, b_fg, zg, zb)


def _mixer_kernel(q_ref, k_ref, v_ref, og_ref, u_ref, z_ref, ga_ref, gb_ref,
                  fg_ref, wf2_ref, bf2_ref, gn_ref, ws_ref, bs_ref,
                  m_ref, st_ref, yb_ref):
    @pl.when(pl.program_id(1) == 0)
    def _():
        st_ref[...] = jnp.zeros_like(st_ref)

    ri = lax.broadcasted_iota(jnp.int32, (GMLP_BLOCK, GMLP_BLOCK), 0) // CHUNK
    ci = lax.broadcasted_iota(jnp.int32, (GMLP_BLOCK, GMLP_BLOCK), 1) // CHUNK
    for g in range(GMLP_GROUPS):
        cs = slice(g * GMLP_DG, (g + 1) * GMLP_DG)
        wm = jnp.where(ri >= ci, ws_ref[g], 0.0).astype(BF16)
        for nb in range(TS2 // GMLP_BLOCK):
            rows = slice(nb * GMLP_BLOCK, (nb + 1) * GMLP_BLOCK)
            sg = jnp.dot(wm, z_ref[rows, cs], preferred_element_type=F32) + bs_ref[:, cs]
            yb_ref[rows, cs] = u_ref[rows, cs].astype(F32) * sg

    fl = jnp.dot(fg_ref[...], wf2_ref[...], preferred_element_type=F32) + bf2_ref[...]
    la = (jnp.minimum(fl, 0.0) - jnp.log1p(jnp.exp(-jnp.abs(fl)))) * (1.0 / GLA_TAU)

    tri = (lax.broadcasted_iota(jnp.int32, (CHUNK, CHUNK), 0)
           >= lax.broadcasted_iota(jnp.int32, (CHUNK, CHUNK), 1)).astype(BF16)

    for c in range(TS2 // CHUNK):
        rows = slice(c * CHUNK, (c + 1) * CHUNK)
        la_c = la[rows, :]
        p0 = la_c.astype(BF16)
        r0 = la_c - p0.astype(F32)
        p1 = r0.astype(BF16)
        p2 = (r0 - p1.astype(F32)).astype(BF16)
        b = (jnp.dot(tri, p0, preferred_element_type=F32)
             + jnp.dot(tri, p1, preferred_element_type=F32)
             + jnp.dot(tri, p2, preferred_element_type=F32))
        bend = b[CHUNK - 1:CHUNK, :]
        kt = (k_ref[rows, :].astype(F32) * jnp.exp(bend - b)).astype(BF16)
        dec = jnp.exp(bend)
        for h in range(GLA_HEADS):
            kc = slice(h * GLA_DK, (h + 1) * GLA_DK)
            vc = slice(h * GLA_DV, (h + 1) * GLA_DV)
            upd = lax.dot_general(v_ref[rows, vc], kt[:, kc],
                                  (((0,), (0,)), ((), ())),
                                  preferred_element_type=F32)
            st = st_ref[h] * dec[:, kc] + upd
            st_ref[h] = st
            o = lax.dot_general(q_ref[rows, kc], st.astype(BF16),
                                (((1,), (1,)), ((), ())),
                                preferred_element_type=F32) * (GLA_DK ** -0.5)
            ms = jnp.mean(o * o, axis=-1, keepdims=True)
            ya = (o * lax.rsqrt(ms + LN_EPS) * gn_ref[:, vc]) * og_ref[rows, vc].astype(F32)
            mm = (ga_ref[rows, vc].astype(F32) * ya
                  + gb_ref[rows, vc].astype(F32) * yb_ref[rows, vc])
            m_ref[rows, vc] = mm.astype(BF16)


def _mixer(proj, fg, wf2, bf2, gn, ws, bs_full, bsz, seq):
    nt = seq // TS2
    row = lambda b, s: b * nt + s
    col = lambda c: (lambda b, s: (row(b, s), c))
    return pl.pallas_call(
        _mixer_kernel,
        grid=(bsz, nt),
        in_specs=[
            pl.BlockSpec((TS2, N_QK), col(0)),
            pl.BlockSpec((TS2, N_QK), col(1)),
            pl.BlockSpec((TS2, D_MODEL), col(1)),
            pl.BlockSpec((TS2, D_MODEL), col(2)),
            pl.BlockSpec((TS2, D_MODEL), col(3)),
            pl.BlockSpec((TS2, D_MODEL), col(4)),
            pl.BlockSpec((TS2, D_MODEL), col(5)),
            pl.BlockSpec((TS2, D_MODEL), col(6)),
            pl.BlockSpec((TS2, FG_PAD), col(0)),
            pl.BlockSpec((FG_PAD, N_QK), lambda b, s: (0, 0)),
            pl.BlockSpec((1, N_QK), lambda b, s: (0, 0)),
            pl.BlockSpec((1, D_MODEL), lambda b, s: (0, 0)),
            pl.BlockSpec((GMLP_GROUPS, GMLP_BLOCK, GMLP_BLOCK), lambda b, s: (0, 0, 0)),
            pl.BlockSpec((GMLP_BLOCK, D_MODEL), lambda b, s: (0, 0)),
        ],
        out_specs=pl.BlockSpec((TS2, D_MODEL), col(0)),
        out_shape=jax.ShapeDtypeStruct((bsz * seq, D_MODEL), BF16),
        scratch_shapes=[
            pltpu.VMEM((GLA_HEADS, GLA_DV, GLA_DK), F32),
            pltpu.VMEM((TS2, D_MODEL), F32),
        ],
        compiler_params=pltpu.CompilerParams(
            dimension_semantics=("arbitrary", "arbitrary"),
            vmem_limit_bytes=VMEM_LIMIT),
        name="mixer",
    )(proj, proj, proj, proj, proj, proj, proj, proj, fg, wf2, bf2, gn, ws, bs_full)


def _outproj_kernel(m_ref, x_ref, p_ref, wo_ref, wpg_ref, wpu_ref,
                    g0_ref, b0_ref, g1_ref, b1_ref, bpg_ref, h1_ref, ple_ref):
    h0 = _ln_rows(x_ref[...], g0_ref[...], b0_ref[...])
    r = ALPHA * h0 + jnp.dot(m_ref[...], wo_ref[...], preferred_element_type=F32)
    h1 = _ln_rows(r, g1_ref[...], b1_ref[...])
    h1_ref[...] = h1
    hb = h1.astype(BF16)
    gate = _sigmoid(jnp.dot(hb, wpg_ref[...], preferred_element_type=F32) + bpg_ref[...])
    pu = jnp.dot(p_ref[...].astype(BF16), wpu_ref[...], preferred_element_type=F32)
    ple_ref[...] = gate * pu


def _outproj(m, x2, p2, wo, wpg, wpu, g0, b0, g1, b1, bpg):
    n = x2.shape[0]
    const = lambda shape: pl.BlockSpec(shape, lambda i: (0, 0))
    return pl.pallas_call(
        _outproj_kernel,
        grid=(n // TM3,),
        in_specs=[
            pl.BlockSpec((TM3, D_MODEL), lambda i: (i, 0)),
            pl.BlockSpec((TM3, D_MODEL), lambda i: (i, 0)),
            pl.BlockSpec((TM3, PLE_DIM), lambda i: (i, 0)),
            const((D_MODEL, D_MODEL)),
            const((D_MODEL, D_MODEL)),
            const((PLE_DIM, D_MODEL)),
            const((1, D_MODEL)), const((1, D_MODEL)),
            const((1, D_MODEL)), const((1, D_MODEL)), const((1, D_MODEL)),
        ],
        out_specs=[
            pl.BlockSpec((TM3, D_MODEL), lambda i: (i, 0)),
            pl.BlockSpec((TM3, D_MODEL), lambda i: (i, 0)),
        ],
        out_shape=[
            jax.ShapeDtypeStruct((n, D_MODEL), F32),
            jax.ShapeDtypeStruct((n, D_MODEL), F32),
        ],
        compiler_params=pltpu.CompilerParams(
            dimension_semantics=("arbitrary",),
            vmem_limit_bytes=VMEM_LIMIT),
        name="outproj",
    )(m, x2, p2, wo, wpg, wpu, g0, b0, g1, b1, bpg)


def _ffn_kernel(h1_ref, ple_ref, wg_ref, wu_ref, wd_ref, g2_ref, b2_ref,
                out_ref, hb_ref, acc_ref):
    f = pl.program_id(1)

    @pl.when(f == 0)
    def _():
        hb_ref[...] = h1_ref[...].astype(BF16)
        acc_ref[...] = jnp.zeros_like(acc_ref)

    hb = hb_ref[...]
    gt = jnp.dot(hb, wg_ref[...], preferred_element_type=F32)
    up = jnp.dot(hb, wu_ref[...], preferred_element_type=F32)
    a = (gt * _sigmoid(gt) * up).astype(BF16)
    acc_ref[...] += jnp.dot(a, wd_ref[...], preferred_element_type=F32)

    @pl.when(f == pl.num_programs(1) - 1)
    def _():
        r = ALPHA * h1_ref[...] + acc_ref[...] + ple_ref[...]
        out_ref[...] = _ln_rows(r, g2_ref[...], b2_ref[...])


def _ffn(h1, ple, wgu, wd, g2, b2):
    n = h1.shape[0]
    nf = D_FF // TF4
    return pl.pallas_call(
        _ffn_kernel,
        grid=(n // TM4, nf),
        in_specs=[
            pl.BlockSpec((TM4, D_MODEL), lambda i, f: (i, 0)),
            pl.BlockSpec((TM4, D_MODEL), lambda i, f: (i, 0)),
            pl.BlockSpec((D_MODEL, TF4), lambda i, f: (0, f)),
            pl.BlockSpec((D_MODEL, TF4), lambda i, f: (0, f + nf)),
            pl.BlockSpec((TF4, D_MODEL), lambda i, f: (f, 0)),
            pl.BlockSpec((1, D_MODEL), lambda i, f: (0, 0)),
            pl.BlockSpec((1, D_MODEL), lambda i, f: (0, 0)),
        ],
        out_specs=pl.BlockSpec((TM4, D_MODEL), lambda i, f: (i, 0)),
        out_shape=jax.ShapeDtypeStruct((n, D_MODEL), F32),
        scratch_shapes=[
            pltpu.VMEM((TM4, D_MODEL), BF16),
            pltpu.VMEM((TM4, D_MODEL), F32),
        ],
        compiler_params=pltpu.CompilerParams(
            dimension_semantics=("arbitrary", "arbitrary"),
            vmem_limit_bytes=VMEM_LIMIT),
        name="ffn",
    )(h1, ple, wgu, wgu, wd, g2, b2)


def kernel(x, p, ln0_g, ln0_b, w_in, b_in, w_f2, b_f2, gla_norm_g, gmlp_ln_g, gmlp_ln_b,
           w_s, b_s, w_o, ln1_g, ln1_b, w_gu, w_down, w_pg, b_pg, w_pu, ln2_g, ln2_b):
    bsz, seq, d = x.shape
    assert d == D_MODEL and w_in.shape[0] == 1
    n = bsz * seq
    assert n % TM1 == 0 and seq % TS2 == 0 and n % TM4 == 0
    row = lambda a: a.reshape(1, -1).astype(F32)

    wi, bi = w_in[0], b_in[0]
    fg0 = 2 * N_QK + 2 * D_MODEL
    fg1 = fg0 + GLA_HEADS * GLA_RANK
    w_main = jnp.concatenate([wi[:, :fg0], wi[:, fg1:]], axis=1).astype(BF16)
    b_main = jnp.concatenate([bi[:fg0], bi[fg1:]]).reshape(1, -1)
    w_fg = jnp.pad(wi[:, fg0:fg1], ((0, 0), (0, FG_PAD - (fg1 - fg0)))).astype(BF16)
    b_fg = jnp.pad(bi[fg0:fg1], (0, FG_PAD - (fg1 - fg0))).reshape(1, -1)
    wf2 = jnp.zeros((FG_PAD, N_QK), F32)
    for h in range(GLA_HEADS):
        wf2 = wf2.at[h * GLA_RANK:(h + 1) * GLA_RANK, h * GLA_DK:(h + 1) * GLA_DK].set(w_f2[0, h])
    wf2 = wf2.astype(BF16)
    bs_full = jnp.repeat(b_s[0].T, GMLP_DG, axis=1)

    x2 = x.reshape(n, d)
    p2 = p[0].reshape(n, PLE_DIM)

    proj, fg = _inproj(x2, row(ln0_g), row(ln0_b), w_main, b_main, w_fg, b_fg,
                       row(gmlp_ln_g[0]), row(gmlp_ln_b[0]))
    m = _mixer(proj, fg, wf2, row(b_f2[0]), row(gla_norm_g[0]), w_s[0], bs_full, bsz, seq)
    h1, ple = _outproj(m, x2, p2, w_o[0].astype(BF16), w_pg[0].astype(BF16),
                       w_pu[0].astype(BF16), row(ln0_g), row(ln0_b),
                       row(ln1_g[0]), row(ln1_b[0]), row(b_pg[0]))
    out = _ffn(h1, ple, w_gu[0].astype(BF16), w_down[0].astype(BF16),
               row(ln2_g[0]), row(ln2_b[0]))
    return out.reshape(bsz, seq, d)
```

```python
import jax
import jax.numpy as jnp
from jax import lax
from jax.experimental import pallas as pl
from jax.experimental.pallas import tpu as pltpu

F32 = jnp.float32
BF16 = jnp.bfloat16

D_MODEL = 2048
CHUNK = 64
GLA_HEADS = 4
GLA_DK = 256
GLA_DV = 512
GLA_RANK = 16
GLA_TAU = 16.0
GMLP_BLOCK = 128
GMLP_GROUPS = 4
GMLP_DG = 512
D_FF = 5632
PLE_DIM = 256
ALPHA = 2.0 ** 0.25
LN_EPS = 1e-5
SQRT_HALF = 0.7071067811865476

N_QK = GLA_HEADS * GLA_DK
N_MAIN = 2 * N_QK + 6 * D_MODEL
FG_COL0 = 2 * N_QK + 2 * D_MODEL
FG_WIDTH = GLA_HEADS * GLA_RANK
FG_PAD = 128

VMEM_LIMIT = 56 * 1024 * 1024

TM1, TN1, RC1 = 1024, 1024, 256
ROWS1 = (256, 256, 256, 128, 128)
N_A = FG_COL0 // TN1
TS2 = 256
TM3, RC3 = 512, 256
TM4, TF4, RC4 = 1024, 512, 256


def _ln_rows(x, g, b):
    mu = jnp.mean(x, axis=-1, keepdims=True)
    xc = x - mu
    var = jnp.mean(xc * xc, axis=-1, keepdims=True)
    return xc * lax.rsqrt(var + LN_EPS) * g + b


def _gelu(x):
    return 0.5 * x * (1.0 + lax.erf(x * SQRT_HALF))


def _sigmoid(x):
    return 0.5 * jnp.tanh(0.5 * x) + 0.5


def _inproj_kernel(x_ref, g0_ref, b0_ref, wa_ref, wb_ref, ba_ref, bb_ref,
                   wfg_ref, bfg_ref, zg_ref, zb_ref, out_ref, fg_ref, hb_ref, tmp_ref):
    j = pl.program_id(1)

    @pl.when(j == 0)
    def _():
        def body(r, c):
            rows = pl.ds(pl.multiple_of(r * RC1, RC1), RC1)
            h = _ln_rows(x_ref[rows, :], g0_ref[...], b0_ref[...])
            hb = h.astype(BF16)
            hb_ref[rows, :] = hb
            fg = jnp.dot(hb, wfg_ref[...], preferred_element_type=F32) + bfg_ref[...]
            fg_ref[rows, :] = fg.astype(BF16)
            return c
        lax.fori_loop(0, TM1 // RC1, body, 0)

    def section(act, w_ref, bias_ref, staged=True):
        r0 = 0
        for ci, nr in enumerate(ROWS1):
            rows = slice(r0, r0 + nr)
            acc = jnp.dot(hb_ref[rows, :], w_ref[...], preferred_element_type=F32) + bias_ref[...]
            if staged:
                tmp_ref[ci % 2, :nr, :] = acc
                acc = tmp_ref[ci % 2, :nr, :]
            out_ref[rows, :] = act(acc).astype(BF16)
            r0 += nr

    def z_act(a):
        halves = []
        for g in range(TN1 // GMLP_DG):
            cs = slice(g * GMLP_DG, (g + 1) * GMLP_DG)
            halves.append(_ln_rows(_gelu(a[:, cs]), zg_ref[:, cs], zb_ref[:, cs]))
        return jnp.concatenate(halves, axis=-1)

    pl.when(j < 4)(lambda: section(lambda a: a, wa_ref, ba_ref, staged=False))
    pl.when((j >= 4) & (j < 6))(lambda: section(lambda a: a * _sigmoid(a), wa_ref, ba_ref))
    pl.when((j >= 6) & (j < 8))(lambda: section(_gelu, wb_ref, bb_ref))
    pl.when((j >= 8) & (j < 10))(lambda: section(z_act, wb_ref, bb_ref))
    pl.when(j >= 10)(lambda: section(_sigmoid, wb_ref, bb_ref))


def _inproj(x2, g0, b0, w_a, w_b, b_a, b_b, zg, zb):
    m = x2.shape[0]
    grid = (m // TM1, N_MAIN // TN1)
    ja = lambda i, j: (0, jnp.minimum(j, N_A - 1))
    jb = lambda i, j: (0, jnp.maximum(j - N_A, 0))
    jz = lambda i, j: (0, jnp.clip(j - 8, 0, 1))
    fgblk = lambda i, j: (0, FG_COL0 // FG_PAD)
    return pl.pallas_call(
        _inproj_kernel,
        grid=grid,
        in_specs=[
            pl.BlockSpec((TM1, D_MODEL), lambda i, j: (i, 0)),
            pl.BlockSpec((1, D_MODEL), lambda i, j: (0, 0)),
            pl.BlockSpec((1, D_MODEL), lambda i, j: (0, 0)),
            pl.BlockSpec((D_MODEL, TN1), ja),
            pl.BlockSpec((D_MODEL, TN1), jb),
            pl.BlockSpec((1, TN1), ja),
            pl.BlockSpec((1, TN1), jb),
            pl.BlockSpec((D_MODEL, FG_PAD), fgblk),
            pl.BlockSpec((1, FG_PAD), fgblk),
            pl.BlockSpec((1, TN1), jz),
            pl.BlockSpec((1, TN1), jz),
        ],
        out_specs=[
            pl.BlockSpec((TM1, TN1), lambda i, j: (i, j)),
            pl.BlockSpec((TM1, FG_PAD), lambda i, j: (i, 0)),
        ],
        out_shape=[
            jax.ShapeDtypeStruct((m, N_MAIN), BF16),
            jax.ShapeDtypeStruct((m, FG_PAD), BF16),
        ],
        scratch_shapes=[pltpu.VMEM((TM1, D_MODEL), BF16),
                        pltpu.VMEM((2, max(ROWS1), TN1), F32)],
        compiler_params=pltpu.CompilerParams(
            dimension_semantics=("arbitrary", "arbitrary"),
            vmem_limit_bytes=VMEM_LIMIT),
        name="inproj",
    )(x2, g0, b0, w_a, w_b, b_a, b_b, w_a, b_a, zg, zb)


def _mixer_kernel(q_ref, k_ref, v_ref, og_ref, u_ref, z_ref, ga_ref, gb_ref,
                  fg_ref, wf2_ref, bf2_ref, gn_ref, ws_ref, bs_ref,
                  m_ref, st_ref, yb_ref):
    @pl.when(pl.program_id(1) == 0)
    def _():
        st_ref[...] = jnp.zeros_like(st_ref)

    ri = lax.broadcasted_iota(jnp.int32, (GMLP_BLOCK, GMLP_BLOCK), 0) // CHUNK
    ci = lax.broadcasted_iota(jnp.int32, (GMLP_BLOCK, GMLP_BLOCK), 1) // CHUNK
    for g in range(GMLP_GROUPS):
        cs = slice(g * GMLP_DG, (g + 1) * GMLP_DG)
        wm = jnp.where(ri >= ci, ws_ref[g], 0.0).astype(BF16)
        for nb in range(TS2 // GMLP_BLOCK):
            rows = slice(nb * GMLP_BLOCK, (nb + 1) * GMLP_BLOCK)
            sg = jnp.dot(wm, z_ref[rows, cs], preferred_element_type=F32) + bs_ref[:, cs]
            yb_ref[rows, cs] = u_ref[rows, cs].astype(F32) * sg

    fl = jnp.dot(fg_ref[...], wf2_ref[...], preferred_element_type=F32) + bf2_ref[...]
    la = (jnp.minimum(fl, 0.0) - jnp.log1p(jnp.exp(-jnp.abs(fl)))) * (1.0 / GLA_TAU)

    tri = (lax.broadcasted_iota(jnp.int32, (CHUNK, CHUNK), 0)
           >= lax.broadcasted_iota(jnp.int32, (CHUNK, CHUNK), 1)).astype(BF16)

    for c in range(TS2 // CHUNK):
        rows = slice(c * CHUNK, (c + 1) * CHUNK)
        la_c = la[rows, :]
        p0 = la_c.astype(BF16)
        r0 = la_c - p0.astype(F32)
        p1 = r0.astype(BF16)
        p2 = (r0 - p1.astype(F32)).astype(BF16)
        b = (jnp.dot(tri, p0, preferred_element_type=F32)
             + jnp.dot(tri, p1, preferred_element_type=F32)
             + jnp.dot(tri, p2, preferred_element_type=F32))
        bend = b[CHUNK - 1:CHUNK, :]
        kt = (k_ref[rows, :].astype(F32) * jnp.exp(bend - b)).astype(BF16)
        dec = jnp.exp(bend)
        for h in range(GLA_HEADS):
            kc = slice(h * GLA_DK, (h + 1) * GLA_DK)
            vc = slice(h * GLA_DV, (h + 1) * GLA_DV)
            upd = lax.dot_general(v_ref[rows, vc], kt[:, kc],
                                  (((0,), (0,)), ((), ())),
                                  preferred_element_type=F32)
            st = st_ref[h] * dec[:, kc] + upd
            st_ref[h] = st
            o = lax.dot_general(q_ref[rows, kc], st.astype(BF16),
                                (((1,), (1,)), ((), ())),
                                preferred_element_type=F32) * (GLA_DK ** -0.5)
            ms = jnp.mean(o * o, axis=-1, keepdims=True)
            ya = (o * lax.rsqrt(ms + LN_EPS) * gn_ref[:, vc]) * og_ref[rows, vc].astype(F32)
            mm = (ga_ref[rows, vc].astype(F32) * ya
                  + gb_ref[rows, vc].astype(F32) * yb_ref[rows, vc])
            m_ref[rows, vc] = mm.astype(BF16)


def _mixer(proj, fg, wf2, bf2, gn, ws, bs_full, bsz, seq):
    nt = seq // TS2
    row = lambda b, s: b * nt + s
    col = lambda c: (lambda b, s: (row(b, s), c))
    return pl.pallas_call(
        _mixer_kernel,
        grid=(bsz, nt),
        in_specs=[
            pl.BlockSpec((TS2, N_QK), col(0)),
            pl.BlockSpec((TS2, N_QK), col(1)),
            pl.BlockSpec((TS2, D_MODEL), col(1)),
            pl.BlockSpec((TS2, D_MODEL), col(2)),
            pl.BlockSpec((TS2, D_MODEL), col(3)),
            pl.BlockSpec((TS2, D_MODEL), col(4)),
            pl.BlockSpec((TS2, D_MODEL), col(5)),
            pl.BlockSpec((TS2, D_MODEL), col(6)),
            pl.BlockSpec((TS2, FG_PAD), col(0)),
            pl.BlockSpec((FG_PAD, N_QK), lambda b, s: (0, 0)),
            pl.BlockSpec((1, N_QK), lambda b, s: (0, 0)),
            pl.BlockSpec((1, D_MODEL), lambda b, s: (0, 0)),
            pl.BlockSpec((GMLP_GROUPS, GMLP_BLOCK, GMLP_BLOCK), lambda b, s: (0, 0, 0)),
            pl.BlockSpec((GMLP_BLOCK, D_MODEL), lambda b, s: (0, 0)),
        ],
        out_specs=pl.BlockSpec((TS2, D_MODEL), col(0)),
        out_shape=jax.ShapeDtypeStruct((bsz * seq, D_MODEL), BF16),
        scratch_shapes=[
            pltpu.VMEM((GLA_HEADS, GLA_DV, GLA_DK), F32),
            pltpu.VMEM((TS2, D_MODEL), F32),
        ],
        compiler_params=pltpu.CompilerParams(
            dimension_semantics=("arbitrary", "arbitrary"),
            vmem_limit_bytes=VMEM_LIMIT),
        name="mixer",
    )(proj, proj, proj, proj, proj, proj, proj, proj, fg, wf2, bf2, gn, ws, bs_full)


def _outproj_kernel(m_ref, x_ref, p_ref, wo_ref, wpg_ref, wpu_ref,
                    g0_ref, b0_ref, g1_ref, b1_ref, bpg_ref, h1_ref, ple_ref):
    def mix(c):
        rows = slice(c * RC3, (c + 1) * RC3)
        h0 = _ln_rows(x_ref[rows, :], g0_ref[...], b0_ref[...])
        r = ALPHA * h0 + jnp.dot(m_ref[rows, :], wo_ref[...], preferred_element_type=F32)
        hb = _ln_rows(r, g1_ref[...], b1_ref[...]).astype(BF16)
        h1_ref[rows, :] = hb
        return hb

    def embed(c, hb):
        rows = slice(c * RC3, (c + 1) * RC3)
        gate = _sigmoid(jnp.dot(hb, wpg_ref[...], preferred_element_type=F32) + bpg_ref[...])
        pu = jnp.dot(p_ref[rows, :].astype(BF16), wpu_ref[...], preferred_element_type=F32)
        ple_ref[rows, :] = (gate * pu).astype(BF16)

    nchunk = TM3 // RC3
    hb_next = mix(0)
    for c in range(nchunk):
        hb = hb_next
        if c + 1 < nchunk:
            hb_next = mix(c + 1)
        embed(c, hb)


def _outproj(m, x2, p2, wo, wpg, wpu, g0, b0, g1, b1, bpg):
    n = x2.shape[0]
    const = lambda shape: pl.BlockSpec(shape, lambda i: (0, 0), pipeline_mode=pl.Buffered(1))
    return pl.pallas_call(
        _outproj_kernel,
        grid=(n // TM3,),
        in_specs=[
            pl.BlockSpec((TM3, D_MODEL), lambda i: (i, 0)),
            pl.BlockSpec((TM3, D_MODEL), lambda i: (i, 0)),
            pl.BlockSpec((TM3, PLE_DIM), lambda i: (i, 0)),
            const((D_MODEL, D_MODEL)),
            const((D_MODEL, D_MODEL)),
            const((PLE_DIM, D_MODEL)),
            const((1, D_MODEL)), const((1, D_MODEL)),
            const((1, D_MODEL)), const((1, D_MODEL)), const((1, D_MODEL)),
        ],
        out_specs=[
            pl.BlockSpec((TM3, D_MODEL), lambda i: (i, 0)),
            pl.BlockSpec((TM3, D_MODEL), lambda i: (i, 0)),
        ],
        out_shape=[
            jax.ShapeDtypeStruct((n, D_MODEL), BF16),
            jax.ShapeDtypeStruct((n, D_MODEL), BF16),
        ],
        compiler_params=pltpu.CompilerParams(
            dimension_semantics=("arbitrary",),
            vmem_limit_bytes=VMEM_LIMIT),
        name="outproj",
    )(m, x2, p2, wo, wpg, wpu, g0, b0, g1, b1, bpg)


def _ffn_kernel(h1_ref, ple_ref, wg_ref, wu_ref, wd_ref, g2_ref, b2_ref, out_ref):
    f = pl.program_id(1)
    last = pl.num_programs(1) - 1

    def gate_up(r):
        hb = h1_ref[r * RC4:(r + 1) * RC4, :]
        gt = jnp.dot(hb, wg_ref[...], preferred_element_type=F32)
        up = jnp.dot(hb, wu_ref[...], preferred_element_type=F32)
        return (gt * _sigmoid(gt) * up).astype(BF16)

    def step(mode):
        nchunk = TM4 // RC4
        a_next = gate_up(0)
        for r in range(nchunk):
            rows = slice(r * RC4, (r + 1) * RC4)
            a = a_next
            if r + 1 < nchunk:
                a_next = gate_up(r + 1)
            d = jnp.dot(a, wd_ref[...], preferred_element_type=F32)
            if mode == "first":
                out_ref[rows, :] = d
            elif mode == "mid":
                out_ref[rows, :] += d
            else:
                res = (ALPHA * h1_ref[rows, :].astype(F32) + (out_ref[rows, :] + d)
                       + ple_ref[rows, :].astype(F32))
                out_ref[rows, :] = _ln_rows(res, g2_ref[...], b2_ref[...])

    pl.when(f == 0)(lambda: step("first"))
    pl.when((f > 0) & (f < last))(lambda: step("mid"))
    pl.when(f == last)(lambda: step("last"))


def _ffn(h1, ple, wgu, wd, g2, b2):
    n = h1.shape[0]
    nf = D_FF // TF4
    return pl.pallas_call(
        _ffn_kernel,
        grid=(n // TM4, nf),
        in_specs=[
            pl.BlockSpec((TM4, D_MODEL), lambda i, f: (i, 0)),
            pl.BlockSpec((TM4, D_MODEL), lambda i, f: (i, 0)),
            pl.BlockSpec((D_MODEL, TF4), lambda i, f: (0, f)),
            pl.BlockSpec((D_MODEL, TF4), lambda i, f: (0, f + nf)),
            pl.BlockSpec((TF4, D_MODEL), lambda i, f: (f, 0)),
            pl.BlockSpec((1, D_MODEL), lambda i, f: (0, 0)),
            pl.BlockSpec((1, D_MODEL), lambda i, f: (0, 0)),
        ],
        out_specs=pl.BlockSpec((TM4, D_MODEL), lambda i, f: (i, 0)),
        out_shape=jax.ShapeDtypeStruct((n, D_MODEL), F32),
        compiler_params=pltpu.CompilerParams(
            dimension_semantics=("arbitrary", "arbitrary"),
            vmem_limit_bytes=VMEM_LIMIT),
        name="ffn",
    )(h1, ple, wgu, wgu, wd, g2, b2)


def kernel(x, p, ln0_g, ln0_b, w_in, b_in, w_f2, b_f2, gla_norm_g, gmlp_ln_g, gmlp_ln_b,
           w_s, b_s, w_o, ln1_g, ln1_b, w_gu, w_down, w_pg, b_pg, w_pu, ln2_g, ln2_b):
    bsz, seq, d = x.shape
    assert d == D_MODEL and w_in.shape[0] == 1
    n = bsz * seq
    assert n % TM1 == 0 and seq % TS2 == 0 and n % TM4 == 0 and n % TM3 == 0
    row = lambda a: a.reshape(1, -1).astype(F32)

    wi, bi = w_in[0], b_in[0]
    col_b = FG_COL0 + FG_WIDTH
    w_a = wi[:, :FG_COL0 + FG_PAD].astype(BF16)
    w_b = wi[:, col_b:].astype(BF16)
    b_a = bi[:FG_COL0 + FG_PAD].reshape(1, -1)
    b_b = bi[col_b:].reshape(1, -1)
    wf2 = jnp.zeros((FG_PAD, N_QK), F32)
    for h in range(GLA_HEADS):
        wf2 = wf2.at[h * GLA_RANK:(h + 1) * GLA_RANK, h * GLA_DK:(h + 1) * GLA_DK].set(w_f2[0, h])
    wf2 = wf2.astype(BF16)
    bs_full = jnp.repeat(b_s[0].T, GMLP_DG, axis=1)

    x2 = x.reshape(n, d)
    p2 = p[0].reshape(n, PLE_DIM)

    proj, fg = _inproj(x2, row(ln0_g), row(ln0_b), w_a, w_b, b_a, b_b,
                       row(gmlp_ln_g[0]), row(gmlp_ln_b[0]))
    m = _mixer(proj, fg, wf2, row(b_f2[0]), row(gla_norm_g[0]), w_s[0], bs_full, bsz, seq)
    h1, ple = _outproj(m, x2, p2, w_o[0].astype(BF16), w_pg[0].astype(BF16),
                       w_pu[0].astype(BF16), row(ln0_g), row(ln0_b),
                       row(ln1_g[0]), row(ln1_b[0]), row(b_pg[0]))
    out = _ffn(h1, ple, w_gu[0].astype(BF16), w_down[0].astype(BF16),
               row(ln2_g[0]), row(ln2_b[0]))
    return out.reshape(bsz, seq, d)
```

```python
import jax
import jax.numpy as jnp
from jax import lax
from jax.experimental import pallas as pl
from jax.experimental.pallas import tpu as pltpu

F32 = jnp.float32
BF16 = jnp.bfloat16

D_MODEL = 2048
CHUNK = 64
GLA_HEADS = 4
GLA_DK = 256
GLA_DV = 512
GLA_RANK = 16
GLA_TAU = 16.0
GMLP_BLOCK = 128
GMLP_GROUPS = 4
GMLP_DG = 512
D_FF = 5632
PLE_DIM = 256
ALPHA = 2.0 ** 0.25
LN_EPS = 1e-5
SQRT_HALF = 0.7071067811865476

N_QK = GLA_HEADS * GLA_DK
N_MAIN = 2 * N_QK + 6 * D_MODEL
FG_COL0 = 2 * N_QK + 2 * D_MODEL
FG_WIDTH = GLA_HEADS * GLA_RANK
FG_PAD = 128

VMEM_LIMIT = 56 * 1024 * 1024

TM0, RC0 = 1024, 256
TM1, TN1, RC1 = 2048, 1024, 256
N_A = FG_COL0 // TN1
TS2 = 256
HEAD_GROUP = 2
TM3, RC3 = 512, 256
TM4, TF4, RC4 = 1024, 512, 256


def _ln_rows(x, g, b):
    mu = jnp.mean(x, axis=-1, keepdims=True)
    xc = x - mu
    var = jnp.mean(xc * xc, axis=-1, keepdims=True)
    return xc * lax.rsqrt(var + LN_EPS) * g + b


def _gelu(x):
    return 0.5 * x * (1.0 + lax.erf(x * SQRT_HALF))


def _sigmoid(x):
    return 0.5 * jnp.tanh(0.5 * x) + 0.5


def _ln0_kernel(x_ref, g0_ref, b0_ref, h0_ref):
    def body(r, c):
        rows = pl.ds(pl.multiple_of(r * RC0, RC0), RC0)
        h0_ref[rows, :] = _ln_rows(x_ref[rows, :], g0_ref[...], b0_ref[...]).astype(BF16)
        return c
    lax.fori_loop(0, TM0 // RC0, body, 0)


def _ln0(x2, g0, b0):
    n = x2.shape[0]
    return pl.pallas_call(
        _ln0_kernel,
        grid=(n // TM0,),
        in_specs=[
            pl.BlockSpec((TM0, D_MODEL), lambda i: (i, 0)),
            pl.BlockSpec((1, D_MODEL), lambda i: (0, 0)),
            pl.BlockSpec((1, D_MODEL), lambda i: (0, 0)),
        ],
        out_specs=pl.BlockSpec((TM0, D_MODEL), lambda i: (i, 0)),
        out_shape=jax.ShapeDtypeStruct((n, D_MODEL), BF16),
        compiler_params=pltpu.CompilerParams(
            dimension_semantics=("arbitrary",),
            vmem_limit_bytes=VMEM_LIMIT),
        name="ln0",
    )(x2, g0, b0)


def _inproj_kernel(hb_ref, wa_ref, wb_ref, ba_ref, bb_ref,
                   wfg_ref, bfg_ref, zg_ref, zb_ref, out_ref, fg_ref, tmp_ref):
    j = pl.program_id(1)
    nchunk = TM1 // RC1

    @pl.when(j == 0)
    def _():
        def body(r, c):
            rows = pl.ds(pl.multiple_of(r * RC1, RC1), RC1)
            fg = jnp.dot(hb_ref[rows, :], wfg_ref[...], preferred_element_type=F32) + bfg_ref[...]
            fg_ref[rows, :] = fg.astype(BF16)
            return c
        lax.fori_loop(0, nchunk, body, 0)

    def section(act, w_ref, bias_ref, staged=True):
        for ci in range(nchunk):
            rows = slice(ci * RC1, (ci + 1) * RC1)
            acc = jnp.dot(hb_ref[rows, :], w_ref[...], preferred_element_type=F32) + bias_ref[...]
            if staged:
                tmp_ref[ci % 2] = acc
                acc = tmp_ref[ci % 2]
            out_ref[rows, :] = act(acc).astype(BF16)

    def z_act(a):
        halves = []
        for g in range(TN1 // GMLP_DG):
            cs = slice(g * GMLP_DG, (g + 1) * GMLP_DG)
            halves.append(_ln_rows(_gelu(a[:, cs]), zg_ref[:, cs], zb_ref[:, cs]))
        return jnp.concatenate(halves, axis=-1)

    pl.when(j < 4)(lambda: section(lambda a: a, wa_ref, ba_ref, staged=False))
    pl.when((j >= 4) & (j < 6))(lambda: section(lambda a: a * _sigmoid(a), wa_ref, ba_ref))
    pl.when((j >= 6) & (j < 8))(lambda: section(_gelu, wb_ref, bb_ref))
    pl.when((j >= 8) & (j < 10))(lambda: section(z_act, wb_ref, bb_ref))
    pl.when(j >= 10)(lambda: section(_sigmoid, wb_ref, bb_ref))


def _inproj(hb, w_a, w_b, b_a, b_b, zg, zb):
    m = hb.shape[0]
    grid = (m // TM1, N_MAIN // TN1)
    ja = lambda i, j: (0, jnp.minimum(j, N_A - 1))
    jb = lambda i, j: (0, jnp.maximum(j - N_A, 0))
    jz = lambda i, j: (0, jnp.clip(j - 8, 0, 1))
    fgblk = lambda i, j: (0, FG_COL0 // FG_PAD)
    return pl.pallas_call(
        _inproj_kernel,
        grid=grid,
        in_specs=[
            pl.BlockSpec((TM1, D_MODEL), lambda i, j: (i, 0)),
            pl.BlockSpec((D_MODEL, TN1), ja),
            pl.BlockSpec((D_MODEL, TN1), jb),
            pl.BlockSpec((1, TN1), ja),
            pl.BlockSpec((1, TN1), jb),
            pl.BlockSpec((D_MODEL, FG_PAD), fgblk),
            pl.BlockSpec((1, FG_PAD), fgblk),
            pl.BlockSpec((1, TN1), jz),
            pl.BlockSpec((1, TN1), jz),
        ],
        out_specs=[
            pl.BlockSpec((TM1, TN1), lambda i, j: (i, j)),
            pl.BlockSpec((TM1, FG_PAD), lambda i, j: (i, 0)),
        ],
        out_shape=[
            jax.ShapeDtypeStruct((m, N_MAIN), BF16),
            jax.ShapeDtypeStruct((m, FG_PAD), BF16),
        ],
        scratch_shapes=[pltpu.VMEM((2, RC1, TN1), F32)],
        compiler_params=pltpu.CompilerParams(
            dimension_semantics=("arbitrary", "arbitrary"),
            vmem_limit_bytes=VMEM_LIMIT),
        name="inproj",
    )(hb, w_a, w_b, b_a, b_b, w_a, b_a, zg, zb)


def _mixer_kernel(q_ref, k_ref, v_ref, og_ref, u_ref, z_ref, ga_ref, gb_ref,
                  fg_ref, wf2_ref, bf2_ref, gn_ref, ws_ref, bs_ref,
                  m_ref, st_ref, yb_ref, kt_ref, dec_ref, stb_ref):
    @pl.when(pl.program_id(1) == 0)
    def _():
        st_ref[...] = jnp.zeros_like(st_ref)

    ri = lax.broadcasted_iota(jnp.int32, (GMLP_BLOCK, GMLP_BLOCK), 0) // CHUNK
    ci = lax.broadcasted_iota(jnp.int32, (GMLP_BLOCK, GMLP_BLOCK), 1) // CHUNK
    for g in range(GMLP_GROUPS):
        cs = slice(g * GMLP_DG, (g + 1) * GMLP_DG)
        wm = jnp.where(ri >= ci, ws_ref[g], 0.0).astype(BF16)
        for nb in range(TS2 // GMLP_BLOCK):
            rows = slice(nb * GMLP_BLOCK, (nb + 1) * GMLP_BLOCK)
            sg = jnp.dot(wm, z_ref[rows, cs], preferred_element_type=F32) + bs_ref[:, cs]
            yb_ref[rows, cs] = u_ref[rows, cs].astype(F32) * sg

    fl = jnp.dot(fg_ref[...], wf2_ref[...], preferred_element_type=F32) + bf2_ref[...]
    la = (jnp.minimum(fl, 0.0) - jnp.log1p(jnp.exp(-jnp.abs(fl)))) * (1.0 / GLA_TAU)

    tri = (lax.broadcasted_iota(jnp.int32, (CHUNK, CHUNK), 0)
           >= lax.broadcasted_iota(jnp.int32, (CHUNK, CHUNK), 1)).astype(BF16)

    heads = range(GLA_HEADS)
    kcs = [slice(h * GLA_DK, (h + 1) * GLA_DK) for h in heads]
    vcs = [slice(h * GLA_DV, (h + 1) * GLA_DV) for h in heads]

    for c in range(TS2 // CHUNK):
        rows = slice(c * CHUNK, (c + 1) * CHUNK)
        la_c = la[rows, :]
        p0 = la_c.astype(BF16)
        r0 = la_c - p0.astype(F32)
        p1 = r0.astype(BF16)
        p2 = (r0 - p1.astype(F32)).astype(BF16)
        b = (jnp.dot(tri, p0, preferred_element_type=F32)
             + jnp.dot(tri, p1, preferred_element_type=F32)
             + jnp.dot(tri, p2, preferred_element_type=F32))
        bend = b[CHUNK - 1:CHUNK, :]
        kt_ref[rows, :] = (k_ref[rows, :].astype(F32) * jnp.exp(bend - b)).astype(BF16)
        dec_ref[c] = jnp.transpose(jnp.broadcast_to(jnp.exp(bend), (GMLP_BLOCK, N_QK)))

    for c in range(TS2 // CHUNK):
        rows = slice(c * CHUNK, (c + 1) * CHUNK)
        for h0 in range(0, GLA_HEADS, HEAD_GROUP):
            grp = range(h0, h0 + HEAD_GROUP)
            upd = {h: lax.dot_general(kt_ref[rows, kcs[h]], v_ref[rows, vcs[h]],
                                      (((0,), (0,)), ((), ())), preferred_element_type=F32)
                   for h in grp}
            for h in grp:
                dcol = jnp.concatenate([dec_ref[c, kcs[h], :]] * (GLA_DV // 128), axis=1)
                st = st_ref[h] * dcol + upd[h]
                st_ref[h] = st
                stb_ref[h] = st.astype(BF16)
            o = {h: jnp.dot(q_ref[rows, kcs[h]], stb_ref[h], preferred_element_type=F32)
                 * (GLA_DK ** -0.5) for h in grp}
            for h in grp:
                vc = vcs[h]
                ms = jnp.mean(o[h] * o[h], axis=-1, keepdims=True)
                ya = (o[h] * lax.rsqrt(ms + LN_EPS) * gn_ref[:, vc]) * og_ref[rows, vc].astype(F32)
                mm = (ga_ref[rows, vc].astype(F32) * ya
                      + gb_ref[rows, vc].astype(F32) * yb_ref[rows, vc])
                m_ref[rows, vc] = mm.astype(BF16)


def _mixer(proj, fg, wf2, bf2, gn, ws, bs_full, bsz, seq):
    nt = seq // TS2
    row = lambda b, s: b * nt + s
    col = lambda c: (lambda b, s: (row(b, s), c))
    return pl.pallas_call(
        _mixer_kernel,
        grid=(bsz, nt),
        in_specs=[
            pl.BlockSpec((TS2, N_QK), col(0)),
            pl.BlockSpec((TS2, N_QK), col(1)),
            pl.BlockSpec((TS2, D_MODEL), col(1)),
            pl.BlockSpec((TS2, D_MODEL), col(2)),
            pl.BlockSpec((TS2, D_MODEL), col(3)),
            pl.BlockSpec((TS2, D_MODEL), col(4)),
            pl.BlockSpec((TS2, D_MODEL), col(5)),
            pl.BlockSpec((TS2, D_MODEL), col(6)),
            pl.BlockSpec((TS2, FG_PAD), col(0)),
            pl.BlockSpec((FG_PAD, N_QK), lambda b, s: (0, 0)),
            pl.BlockSpec((1, N_QK), lambda b, s: (0, 0)),
            pl.BlockSpec((1, D_MODEL), lambda b, s: (0, 0)),
            pl.BlockSpec((GMLP_GROUPS, GMLP_BLOCK, GMLP_BLOCK), lambda b, s: (0, 0, 0)),
            pl.BlockSpec((GMLP_BLOCK, D_MODEL), lambda b, s: (0, 0)),
        ],
        out_specs=pl.BlockSpec((TS2, D_MODEL), col(0)),
        out_shape=jax.ShapeDtypeStruct((bsz * seq, D_MODEL), BF16),
        scratch_shapes=[
            pltpu.VMEM((GLA_HEADS, GLA_DK, GLA_DV), F32),
            pltpu.VMEM((TS2, D_MODEL), F32),
            pltpu.VMEM((TS2, N_QK), BF16),
            pltpu.VMEM((TS2 // CHUNK, N_QK, GMLP_BLOCK), F32),
            pltpu.VMEM((GLA_HEADS, GLA_DK, GLA_DV), BF16),
        ],
        compiler_params=pltpu.CompilerParams(
            dimension_semantics=("arbitrary", "arbitrary"),
            vmem_limit_bytes=VMEM_LIMIT),
        name="mixer",
    )(proj, proj, proj, proj, proj, proj, proj, proj, fg, wf2, bf2, gn, ws, bs_full)


def _outproj_kernel(m_ref, h0_ref, p_ref, wo_ref, wpg_ref, wpu_ref,
                    g1_ref, b1_ref, bpg_ref, h1_ref, ple_ref):
    def mix(c):
        rows = slice(c * RC3, (c + 1) * RC3)
        r = (ALPHA * h0_ref[rows, :].astype(F32)
             + jnp.dot(m_ref[rows, :], wo_ref[...], preferred_element_type=F32))
        hb = _ln_rows(r, g1_ref[...], b1_ref[...]).astype(BF16)
        h1_ref[rows, :] = hb
        return hb

    def embed(c, hb):
        rows = slice(c * RC3, (c + 1) * RC3)
        gate = _sigmoid(jnp.dot(hb, wpg_ref[...], preferred_element_type=F32) + bpg_ref[...])
        pu = jnp.dot(p_ref[rows, :].astype(BF16), wpu_ref[...], preferred_element_type=F32)
        ple_ref[rows, :] = (gate * pu).astype(BF16)

    nchunk = TM3 // RC3
    hb_next = mix(0)
    for c in range(nchunk):
        hb = hb_next
        if c + 1 < nchunk:
            hb_next = mix(c + 1)
        embed(c, hb)


def _outproj(m, h0, p2, wo, wpg, wpu, g1, b1, bpg):
    n = h0.shape[0]
    const = lambda shape: pl.BlockSpec(shape, lambda i: (0, 0), pipeline_mode=pl.Buffered(1))
    return pl.pallas_call(
        _outproj_kernel,
        grid=(n // TM3,),
        in_specs=[
            pl.BlockSpec((TM3, D_MODEL), lambda i: (i, 0)),
            pl.BlockSpec((TM3, D_MODEL), lambda i: (i, 0)),
            pl.BlockSpec((TM3, PLE_DIM), lambda i: (i, 0)),
            const((D_MODEL, D_MODEL)),
            const((D_MODEL, D_MODEL)),
            const((PLE_DIM, D_MODEL)),
            const((1, D_MODEL)), const((1, D_MODEL)), const((1, D_MODEL)),
        ],
        out_specs=[
            pl.BlockSpec((TM3, D_MODEL), lambda i: (i, 0)),
            pl.BlockSpec((TM3, D_MODEL), lambda i: (i, 0)),
        ],
        out_shape=[
            jax.ShapeDtypeStruct((n, D_MODEL), BF16),
            jax.ShapeDtypeStruct((n, D_MODEL), BF16),
        ],
        compiler_params=pltpu.CompilerParams(
            dimension_semantics=("arbitrary",),
            vmem_limit_bytes=VMEM_LIMIT),
        name="outproj",
    )(m, h0, p2, wo, wpg, wpu, g1, b1, bpg)


def _ffn_kernel(h1_ref, ple_ref, wg_ref, wu_ref, wd_ref, g2_ref, b2_ref, out_ref):
    f = pl.program_id(1)
    last = pl.num_programs(1) - 1

    def gate_up(r):
        hb = h1_ref[r * RC4:(r + 1) * RC4, :]
        gt = jnp.dot(hb, wg_ref[...], preferred_element_type=F32)
        up = jnp.dot(hb, wu_ref[...], preferred_element_type=F32)
        return (gt * _sigmoid(gt) * up).astype(BF16)

    def step(mode):
        nchunk = TM4 // RC4
        a_next = gate_up(0)
        for r in range(nchunk):
            rows = slice(r * RC4, (r + 1) * RC4)
            a = a_next
            if r + 1 < nchunk:
                a_next = gate_up(r + 1)
            d = jnp.dot(a, wd_ref[...], preferred_element_type=F32)
            if mode == "first":
                out_ref[rows, :] = d
            elif mode == "mid":
                out_ref[rows, :] += d
            else:
                res = (ALPHA * h1_ref[rows, :].astype(F32) + (out_ref[rows, :] + d)
                       + ple_ref[rows, :].astype(F32))
                out_ref[rows, :] = _ln_rows(res, g2_ref[...], b2_ref[...])

    pl.when(f == 0)(lambda: step("first"))
    pl.when((f > 0) & (f < last))(lambda: step("mid"))
    pl.when(f == last)(lambda: step("last"))


def _ffn(h1, ple, wgu, wd, g2, b2):
    n = h1.shape[0]
    nf = D_FF // TF4
    return pl.pallas_call(
        _ffn_kernel,
        grid=(n // TM4, nf),
        in_specs=[
            pl.BlockSpec((TM4, D_MODEL), lambda i, f: (i, 0)),
            pl.BlockSpec((TM4, D_MODEL), lambda i, f: (i, 0)),
            pl.BlockSpec((D_MODEL, TF4), lambda i, f: (0, f)),
            pl.BlockSpec((D_MODEL, TF4), lambda i, f: (0, f + nf)),
            pl.BlockSpec((TF4, D_MODEL), lambda i, f: (f, 0)),
            pl.BlockSpec((1, D_MODEL), lambda i, f: (0, 0)),
            pl.BlockSpec((1, D_MODEL), lambda i, f: (0, 0)),
        ],
        out_specs=pl.BlockSpec((TM4, D_MODEL), lambda i, f: (i, 0)),
        out_shape=jax.ShapeDtypeStruct((n, D_MODEL), F32),
        compiler_params=pltpu.CompilerParams(
            dimension_semantics=("arbitrary", "arbitrary"),
            vmem_limit_bytes=VMEM_LIMIT),
        name="ffn",
    )(h1, ple, wgu, wgu, wd, g2, b2)


def kernel(x, p, ln0_g, ln0_b, w_in, b_in, w_f2, b_f2, gla_norm_g, gmlp_ln_g, gmlp_ln_b,
           w_s, b_s, w_o, ln1_g, ln1_b, w_gu, w_down, w_pg, b_pg, w_pu, ln2_g, ln2_b):
    bsz, seq, d = x.shape
    assert d == D_MODEL and w_in.shape[0] == 1
    n = bsz * seq
    assert n % TM1 == 0 and seq % TS2 == 0 and n % TM4 == 0 and n % TM3 == 0 and n % TM0 == 0
    row = lambda a: a.reshape(1, -1).astype(F32)

    wi, bi = w_in[0], b_in[0]
    col_b = FG_COL0 + FG_WIDTH
    w_a = wi[:, :FG_COL0 + FG_PAD].astype(BF16)
    w_b = wi[:, col_b:].astype(BF16)
    b_a = bi[:FG_COL0 + FG_PAD].reshape(1, -1)
    b_b = bi[col_b:].reshape(1, -1)
    wf2 = jnp.zeros((FG_PAD, N_QK), F32)
    for h in range(GLA_HEADS):
        wf2 = wf2.at[h * GLA_RANK:(h + 1) * GLA_RANK, h * GLA_DK:(h + 1) * GLA_DK].set(w_f2[0, h])
    wf2 = wf2.astype(BF16)
    bs_full = jnp.repeat(b_s[0].T, GMLP_DG, axis=1)

    x2 = x.reshape(n, d)
    p2 = p[0].reshape(n, PLE_DIM)

    h0 = _ln0(x2, row(ln0_g), row(ln0_b))
    proj, fg = _inproj(h0, w_a, w_b, b_a, b_b, row(gmlp_ln_g[0]), row(gmlp_ln_b[0]))
    m = _mixer(proj, fg, wf2, row(b_f2[0]), row(gla_norm_g[0]), w_s[0], bs_full, bsz, seq)
    h1, ple = _outproj(m, h0, p2, w_o[0].astype(BF16), w_pg[0].astype(BF16),
                       w_pu[0].astype(BF16), row(ln1_g[0]), row(ln1_b[0]), row(b_pg[0]))
    out = _ffn(h1, ple, w_gu[0].astype(BF16), w_down[0].astype(BF16),
               row(ln2_g[0]), row(ln2_b[0]))
    return out.reshape(bsz, seq, d)
```

```python
import functools

import jax
import jax.numpy as jnp
from jax import lax
from jax.experimental import pallas as pl
from jax.experimental.pallas import tpu as pltpu

F32 = jnp.float32
BF16 = jnp.bfloat16

D_MODEL = 2048
CHUNK = 64
GLA_HEADS = 4
GLA_DK = 256
GLA_DV = 512
GLA_RANK = 16
GLA_TAU = 16.0
GMLP_BLOCK = 128
GMLP_GROUPS = 4
GMLP_DG = 512
D_FF = 5632
PLE_DIM = 256
ALPHA = 2.0 ** 0.25
LN_EPS = 1e-5
SQRT_HALF = 0.7071067811865476

N_QK = GLA_HEADS * GLA_DK
FG_COL0 = 2 * N_QK + 2 * D_MODEL
FG_WIDTH = GLA_HEADS * GLA_RANK
FG_PAD = 128

VMEM_LIMIT = 56 * 1024 * 1024

TM0, RC0 = 1024, 256
TM1, RC1 = 1024, 256
TS2 = 256
HEAD_GROUP = 2
TM3, RC3 = 512, 256
TM4, TF4, RC4 = 1024, 512, 256


def _ln_rows(x, g, b):
    mu = jnp.mean(x, axis=-1, keepdims=True)
    xc = x - mu
    var = jnp.mean(xc * xc, axis=-1, keepdims=True)
    return xc * lax.rsqrt(var + LN_EPS) * g + b


def _gelu(x):
    return 0.5 * x * (1.0 + lax.erf(x * SQRT_HALF))


def _sigmoid(x):
    return 0.5 * jnp.tanh(0.5 * x) + 0.5


def _ln0_kernel(x_ref, g0_ref, b0_ref, h0_ref):
    def body(r, c):
        rows = pl.ds(pl.multiple_of(r * RC0, RC0), RC0)
        h0_ref[rows, :] = _ln_rows(x_ref[rows, :], g0_ref[...], b0_ref[...]).astype(BF16)
        return c
    lax.fori_loop(0, TM0 // RC0, body, 0)


def _ln0(x2, g0, b0):
    n = x2.shape[0]
    return pl.pallas_call(
        _ln0_kernel,
        grid=(n // TM0,),
        in_specs=[
            pl.BlockSpec((TM0, D_MODEL), lambda i: (i, 0)),
            pl.BlockSpec((1, D_MODEL), lambda i: (0, 0)),
            pl.BlockSpec((1, D_MODEL), lambda i: (0, 0)),
        ],
        out_specs=pl.BlockSpec((TM0, D_MODEL), lambda i: (i, 0)),
        out_shape=jax.ShapeDtypeStruct((n, D_MODEL), BF16),
        compiler_params=pltpu.CompilerParams(
            dimension_semantics=("arbitrary",),
            vmem_limit_bytes=VMEM_LIMIT),
        name="ln0",
    )(x2, g0, b0)


def _proj_kernel(nsec, act, staged, with_fg, hb_ref, w_ref, b_ref, zg_ref, zb_ref, *rest):
    if with_fg:
        wfg_ref, bfg_ref = rest[:2]
        rest = rest[2:]
    out_refs, tmp_ref = rest[:nsec], rest[-1]
    s = pl.program_id(0)

    def run(out_ref, do_fg):
        for ci in range(TM1 // RC1):
            rows = slice(ci * RC1, (ci + 1) * RC1)
            hb = hb_ref[rows, :]
            acc = jnp.dot(hb, w_ref[...], preferred_element_type=F32) + b_ref[...]
            if staged:
                tmp_ref[ci % 2] = acc
                acc = tmp_ref[ci % 2]
            out_ref[rows, :] = act(acc, zg_ref, zb_ref).astype(BF16)
            if do_fg:
                fg = jnp.dot(hb, wfg_ref[...], preferred_element_type=F32) + bfg_ref[...]
                rest[nsec][rows, :] = fg.astype(BF16)

    if nsec == 1:
        run(out_refs[0], with_fg)
    else:
        for k in range(nsec):
            pl.when(s == k)(functools.partial(run, out_refs[k], with_fg and k == 0))


def _act_id(a, zg_ref, zb_ref):
    return a


def _act_silu(a, zg_ref, zb_ref):
    return a * _sigmoid(a)


def _act_gelu(a, zg_ref, zb_ref):
    return _gelu(a)


def _act_sigmoid(a, zg_ref, zb_ref):
    return _sigmoid(a)


def _act_gelu_groupln(a, zg_ref, zb_ref):
    parts = []
    for g in range(GMLP_GROUPS):
        cs = slice(g * GMLP_DG, (g + 1) * GMLP_DG)
        parts.append(_ln_rows(_gelu(a[:, cs]), zg_ref[:, cs], zb_ref[:, cs]))
    return jnp.concatenate(parts, axis=-1)


def _proj(hb, w, bias, blk0, nsec, act, staged, zg, zb, name, fg_block=None):
    n = hb.shape[0]
    nt = n // TM1
    with_fg = fg_block is not None

    def out_idx(k):
        return lambda s, i: (jnp.where(s == k, i, jnp.where(s > k, nt - 1, 0)), 0)

    in_specs = [
        pl.BlockSpec((TM1, D_MODEL), lambda s, i: (i, 0)),
        pl.BlockSpec((D_MODEL, D_MODEL), lambda s, i: (0, blk0 + s)),
        pl.BlockSpec((1, D_MODEL), lambda s, i: (0, blk0 + s)),
        pl.BlockSpec((1, D_MODEL), lambda s, i: (0, 0)),
        pl.BlockSpec((1, D_MODEL), lambda s, i: (0, 0)),
    ]
    operands = [hb, w, bias, zg, zb]
    out_specs = [pl.BlockSpec((TM1, D_MODEL), out_idx(k)) for k in range(nsec)]
    out_shape = [jax.ShapeDtypeStruct((n, D_MODEL), BF16) for _ in range(nsec)]
    if with_fg:
        in_specs += [pl.BlockSpec((D_MODEL, FG_PAD), lambda s, i: (0, fg_block)),
                     pl.BlockSpec((1, FG_PAD), lambda s, i: (0, fg_block))]
        operands += [w, bias]
        out_specs.append(pl.BlockSpec((TM1, FG_PAD), out_idx(0)))
        out_shape.append(jax.ShapeDtypeStruct((n, FG_PAD), BF16))
    return pl.pallas_call(
        functools.partial(_proj_kernel, nsec, act, staged, with_fg),
        grid=(nsec, nt),
        in_specs=in_specs,
        out_specs=out_specs,
        out_shape=out_shape,
        scratch_shapes=[pltpu.VMEM((2, RC1, D_MODEL), F32)],
        compiler_params=pltpu.CompilerParams(
            dimension_semantics=("arbitrary", "arbitrary"),
            vmem_limit_bytes=VMEM_LIMIT),
        name=name,
    )(*operands)


def _mixer_kernel(qk_ref, v_ref, og_ref, u_ref, z_ref, ga_ref, gb_ref,
                  fg_ref, wf2_ref, bf2_ref, gn_ref, ws_ref, bs_ref,
                  m_ref, st_ref, yb_ref, kt_ref, dec_ref, stb_ref):
    @pl.when(pl.program_id(1) == 0)
    def _():
        st_ref[...] = jnp.zeros_like(st_ref)

    ri = lax.broadcasted_iota(jnp.int32, (GMLP_BLOCK, GMLP_BLOCK), 0) // CHUNK
    ci = lax.broadcasted_iota(jnp.int32, (GMLP_BLOCK, GMLP_BLOCK), 1) // CHUNK
    for g in range(GMLP_GROUPS):
        cs = slice(g * GMLP_DG, (g + 1) * GMLP_DG)
        wm = jnp.where(ri >= ci, ws_ref[g], 0.0).astype(BF16)
        for nb in range(TS2 // GMLP_BLOCK):
            rows = slice(nb * GMLP_BLOCK, (nb + 1) * GMLP_BLOCK)
            sg = jnp.dot(wm, z_ref[rows, cs], preferred_element_type=F32) + bs_ref[:, cs]
            yb_ref[rows, cs] = u_ref[rows, cs].astype(F32) * sg

    fl = jnp.dot(fg_ref[...], wf2_ref[...], preferred_element_type=F32) + bf2_ref[...]
    la = (jnp.minimum(fl, 0.0) - jnp.log1p(jnp.exp(-jnp.abs(fl)))) * (1.0 / GLA_TAU)

    tri = (lax.broadcasted_iota(jnp.int32, (CHUNK, CHUNK), 0)
           >= lax.broadcasted_iota(jnp.int32, (CHUNK, CHUNK), 1)).astype(BF16)

    heads = range(GLA_HEADS)
    kcs = [slice(h * GLA_DK, (h + 1) * GLA_DK) for h in heads]
    vcs = [slice(h * GLA_DV, (h + 1) * GLA_DV) for h in heads]

    for c in range(TS2 // CHUNK):
        rows = slice(c * CHUNK, (c + 1) * CHUNK)
        la_c = la[rows, :]
        p0 = la_c.astype(BF16)
        r0 = la_c - p0.astype(F32)
        p1 = r0.astype(BF16)
        p2 = (r0 - p1.astype(F32)).astype(BF16)
        b = (jnp.dot(tri, p0, preferred_element_type=F32)
             + jnp.dot(tri, p1, preferred_element_type=F32)
             + jnp.dot(tri, p2, preferred_element_type=F32))
        bend = b[CHUNK - 1:CHUNK, :]
        kt_ref[rows, :] = (qk_ref[rows, N_QK:].astype(F32) * jnp.exp(bend - b)).astype(BF16)
        dec_ref[c] = jnp.transpose(jnp.broadcast_to(jnp.exp(bend), (GMLP_BLOCK, N_QK)))

    for c in range(TS2 // CHUNK):
        rows = slice(c * CHUNK, (c + 1) * CHUNK)
        for h0 in range(0, GLA_HEADS, HEAD_GROUP):
            grp = range(h0, h0 + HEAD_GROUP)
            upd = {h: lax.dot_general(kt_ref[rows, kcs[h]], v_ref[rows, vcs[h]],
                                      (((0,), (0,)), ((), ())), preferred_element_type=F32)
                   for h in grp}
            for h in grp:
                dcol = jnp.concatenate([dec_ref[c, kcs[h], :]] * (GLA_DV // 128), axis=1)
                st = st_ref[h] * dcol + upd[h]
                st_ref[h] = st
                stb_ref[h] = st.astype(BF16)
            o = {h: jnp.dot(qk_ref[rows, kcs[h]], stb_ref[h], preferred_element_type=F32)
                 * (GLA_DK ** -0.5) for h in grp}
            for h in grp:
                vc = vcs[h]
                ms = jnp.mean(o[h] * o[h], axis=-1, keepdims=True)
                ya = (o[h] * lax.rsqrt(ms + LN_EPS) * gn_ref[:, vc]) * og_ref[rows, vc].astype(F32)
                mm = (ga_ref[rows, vc].astype(F32) * ya
                      + gb_ref[rows, vc].astype(F32) * yb_ref[rows, vc])
                m_ref[rows, vc] = mm.astype(BF16)


def _mixer(qk, v, og, u, z, ga, gb, fg, wf2, bf2, gn, ws, bs_full, bsz, seq):
    nt = seq // TS2
    tile = lambda b, s: (b * nt + s, 0)
    return pl.pallas_call(
        _mixer_kernel,
        grid=(bsz, nt),
        in_specs=[
            pl.BlockSpec((TS2, D_MODEL), tile),
            pl.BlockSpec((TS2, D_MODEL), tile),
            pl.BlockSpec((TS2, D_MODEL), tile),
            pl.BlockSpec((TS2, D_MODEL), tile),
            pl.BlockSpec((TS2, D_MODEL), tile),
            pl.BlockSpec((TS2, D_MODEL), tile),
            pl.BlockSpec((TS2, D_MODEL), tile),
            pl.BlockSpec((TS2, FG_PAD), tile),
            pl.BlockSpec((FG_PAD, N_QK), lambda b, s: (0, 0)),
            pl.BlockSpec((1, N_QK), lambda b, s: (0, 0)),
            pl.BlockSpec((1, D_MODEL), lambda b, s: (0, 0)),
            pl.BlockSpec((GMLP_GROUPS, GMLP_BLOCK, GMLP_BLOCK), lambda b, s: (0, 0, 0)),
            pl.BlockSpec((GMLP_BLOCK, D_MODEL), lambda b, s: (0, 0)),
        ],
        out_specs=pl.BlockSpec((TS2, D_MODEL), tile),
        out_shape=jax.ShapeDtypeStruct((bsz * seq, D_MODEL), BF16),
        scratch_shapes=[
            pltpu.VMEM((GLA_HEADS, GLA_DK, GLA_DV), F32),
            pltpu.VMEM((TS2, D_MODEL), F32),
            pltpu.VMEM((TS2, N_QK), BF16),
            pltpu.VMEM((TS2 // CHUNK, N_QK, GMLP_BLOCK), F32),
            pltpu.VMEM((GLA_HEADS, GLA_DK, GLA_DV), BF16),
        ],
        compiler_params=pltpu.CompilerParams(
            dimension_semantics=("arbitrary", "arbitrary"),
            vmem_limit_bytes=VMEM_LIMIT),
        name="mixer",
    )(qk, v, og, u, z, ga, gb, fg, wf2, bf2, gn, ws, bs_full)


def _outproj_kernel(m_ref, h0_ref, p_ref, wo_ref, wpg_ref, wpu_ref,
                    g1_ref, b1_ref, bpg_ref, h1_ref, ple_ref):
    def mix(c):
        rows = slice(c * RC3, (c + 1) * RC3)
        r = (ALPHA * h0_ref[rows, :].astype(F32)
             + jnp.dot(m_ref[rows, :], wo_ref[...], preferred_element_type=F32))
        hb = _ln_rows(r, g1_ref[...], b1_ref[...]).astype(BF16)
        h1_ref[rows, :] = hb
        return hb

    def embed(c, hb):
        rows = slice(c * RC3, (c + 1) * RC3)
        gate = _sigmoid(jnp.dot(hb, wpg_ref[...], preferred_element_type=F32) + bpg_ref[...])
        pu = jnp.dot(p_ref[rows, :].astype(BF16), wpu_ref[...], preferred_element_type=F32)
        ple_ref[rows, :] = (gate * pu).astype(BF16)

    nchunk = TM3 // RC3
    hb_next = mix(0)
    for c in range(nchunk):
        hb = hb_next
        if c + 1 < nchunk:
            hb_next = mix(c + 1)
        embed(c, hb)


def _outproj(m, h0, p2, wo, wpg, wpu, g1, b1, bpg):
    n = h0.shape[0]
    const = lambda shape: pl.BlockSpec(shape, lambda i: (0, 0), pipeline_mode=pl.Buffered(1))
    return pl.pallas_call(
        _outproj_kernel,
        grid=(n // TM3,),
        in_specs=[
            pl.BlockSpec((TM3, D_MODEL), lambda i: (i, 0)),
            pl.BlockSpec((TM3, D_MODEL), lambda i: (i, 0)),
            pl.BlockSpec((TM3, PLE_DIM), lambda i: (i, 0)),
            const((D_MODEL, D_MODEL)),
            const((D_MODEL, D_MODEL)),
            const((PLE_DIM, D_MODEL)),
            const((1, D_MODEL)), const((1, D_MODEL)), const((1, D_MODEL)),
        ],
        out_specs=[
            pl.BlockSpec((TM3, D_MODEL), lambda i: (i, 0)),
            pl.BlockSpec((TM3, D_MODEL), lambda i: (i, 0)),
        ],
        out_shape=[
            jax.ShapeDtypeStruct((n, D_MODEL), BF16),
            jax.ShapeDtypeStruct((n, D_MODEL), BF16),
        ],
        compiler_params=pltpu.CompilerParams(
            dimension_semantics=("arbitrary",),
            vmem_limit_bytes=VMEM_LIMIT),
        name="outproj",
    )(m, h0, p2, wo, wpg, wpu, g1, b1, bpg)


def _ffn_kernel(h1_ref, ple_ref, wg_ref, wu_ref, wd_ref, g2_ref, b2_ref, out_ref):
    f = pl.program_id(1)
    last = pl.num_programs(1) - 1

    def gate_up(r):
        hb = h1_ref[r * RC4:(r + 1) * RC4, :]
        gt = jnp.dot(hb, wg_ref[...], preferred_element_type=F32)
        up = jnp.dot(hb, wu_ref[...], preferred_element_type=F32)
        return (gt * _sigmoid(gt) * up).astype(BF16)

    def step(mode):
        nchunk = TM4 // RC4
        a_next = gate_up(0)
        for r in range(nchunk):
            rows = slice(r * RC4, (r + 1) * RC4)
            a = a_next
            if r + 1 < nchunk:
                a_next = gate_up(r + 1)
            d = jnp.dot(a, wd_ref[...], preferred_element_type=F32)
            if mode == "first":
                out_ref[rows, :] = d
            elif mode == "mid":
                out_ref[rows, :] += d
            else:
                res = (ALPHA * h1_ref[rows, :].astype(F32) + (out_ref[rows, :] + d)
                       + ple_ref[rows, :].astype(F32))
                out_ref[rows, :] = _ln_rows(res, g2_ref[...], b2_ref[...])

    pl.when(f == 0)(lambda: step("first"))
    pl.when((f > 0) & (f < last))(lambda: step("mid"))
    pl.when(f == last)(lambda: step("last"))


def _ffn(h1, ple, wgu, wd, g2, b2):
    n = h1.shape[0]
    nf = D_FF // TF4
    return pl.pallas_call(
        _ffn_kernel,
        grid=(n // TM4, nf),
        in_specs=[
            pl.BlockSpec((TM4, D_MODEL), lambda i, f: (i, 0)),
            pl.BlockSpec((TM4, D_MODEL), lambda i, f: (i, 0)),
            pl.BlockSpec((D_MODEL, TF4), lambda i, f: (0, f)),
            pl.BlockSpec((D_MODEL, TF4), lambda i, f: (0, f + nf)),
            pl.BlockSpec((TF4, D_MODEL), lambda i, f: (f, 0)),
            pl.BlockSpec((1, D_MODEL), lambda i, f: (0, 0)),
            pl.BlockSpec((1, D_MODEL), lambda i, f: (0, 0)),
        ],
        out_specs=pl.BlockSpec((TM4, D_MODEL), lambda i, f: (i, 0)),
        out_shape=jax.ShapeDtypeStruct((n, D_MODEL), F32),
        compiler_params=pltpu.CompilerParams(
            dimension_semantics=("arbitrary", "arbitrary"),
            vmem_limit_bytes=VMEM_LIMIT),
        name="ffn",
    )(h1, ple, wgu, wgu, wd, g2, b2)


def kernel(x, p, ln0_g, ln0_b, w_in, b_in, w_f2, b_f2, gla_norm_g, gmlp_ln_g, gmlp_ln_b,
           w_s, b_s, w_o, ln1_g, ln1_b, w_gu, w_down, w_pg, b_pg, w_pu, ln2_g, ln2_b):
    bsz, seq, d = x.shape
    assert d == D_MODEL and w_in.shape[0] == 1
    n = bsz * seq
    assert n % TM1 == 0 and seq % TS2 == 0 and n % TM4 == 0 and n % TM3 == 0 and n % TM0 == 0
    row = lambda a: a.reshape(1, -1).astype(F32)

    wi, bi = w_in[0], b_in[0]
    col_b = FG_COL0 + FG_WIDTH
    w_a = wi[:, :FG_COL0 + FG_PAD].astype(BF16)
    w_b = wi[:, col_b:].astype(BF16)
    b_a = bi[:FG_COL0 + FG_PAD].reshape(1, -1)
    b_b = bi[col_b:].reshape(1, -1)
    wf2 = jnp.zeros((FG_PAD, N_QK), F32)
    for h in range(GLA_HEADS):
        wf2 = wf2.at[h * GLA_RANK:(h + 1) * GLA_RANK, h * GLA_DK:(h + 1) * GLA_DK].set(w_f2[0, h])
    wf2 = wf2.astype(BF16)
    bs_full = jnp.repeat(b_s[0].T, GMLP_DG, axis=1)

    x2 = x.reshape(n, d)
    p2 = p[0].reshape(n, PLE_DIM)

    h0 = _ln0(x2, row(ln0_g), row(ln0_b))
    zg, zb = row(gmlp_ln_g[0]), row(gmlp_ln_b[0])
    fg_blk = FG_COL0 // FG_PAD
    qk, v, fg = _proj(h0, w_a, b_a, 0, 2, _act_id, False, zg, zb, "proj_qkv", fg_block=fg_blk)
    og, = _proj(h0, w_a, b_a, 2, 1, _act_silu, True, zg, zb, "proj_og")
    u, = _proj(h0, w_b, b_b, 0, 1, _act_gelu, True, zg, zb, "proj_u")
    z, = _proj(h0, w_b, b_b, 1, 1, _act_gelu_groupln, True, zg, zb, "proj_z")
    ga, gb = _proj(h0, w_b, b_b, 2, 2, _act_sigmoid, True, zg, zb, "proj_gates")
    m = _mixer(qk, v, og, u, z, ga, gb, fg, wf2, row(b_f2[0]), row(gla_norm_g[0]), w_s[0],
               bs_full, bsz, seq)
    h1, ple = _outproj(m, h0, p2, w_o[0].astype(BF16), w_pg[0].astype(BF16),
                       w_pu[0].astype(BF16), row(ln1_g[0]), row(ln1_b[0]), row(b_pg[0]))
    out = _ffn(h1, ple, w_gu[0].astype(BF16), w_down[0].astype(BF16),
               row(ln2_g[0]), row(ln2_b[0]))
    return out.reshape(bsz, seq, d)
```

```python
import functools

import jax
import jax.numpy as jnp
from jax import lax
from jax.experimental import pallas as pl
from jax.experimental.pallas import tpu as pltpu

F32 = jnp.float32
BF16 = jnp.bfloat16

D_MODEL = 2048
CHUNK = 64
GLA_HEADS = 4
GLA_DK = 256
GLA_DV = 512
GLA_RANK = 16
GLA_TAU = 16.0
GMLP_BLOCK = 128
GMLP_GROUPS = 4
GMLP_DG = 512
D_FF = 5632
PLE_DIM = 256
ALPHA = 2.0 ** 0.25
LN_EPS = 1e-5
SQRT_HALF = 0.7071067811865476

N_QK = GLA_HEADS * GLA_DK
FG_COL0 = 2 * N_QK + 2 * D_MODEL
FG_WIDTH = GLA_HEADS * GLA_RANK
FG_PAD = 128

VMEM_LIMIT = 56 * 1024 * 1024

TMQ, RCQ = 512, 256
TM1, RC1 = 1024, 256
TS2 = 256
HEAD_GROUP = 2
TM3, RC3 = 512, 256
TM4, TF4, RC4 = 1024, 512, 256


def _ln_rows(x, g, b):
    mu = jnp.mean(x, axis=-1, keepdims=True)
    xc = x - mu
    var = jnp.mean(xc * xc, axis=-1, keepdims=True)
    return xc * lax.rsqrt(var + LN_EPS) * g + b


def _gelu(x):
    return 0.5 * x * (1.0 + lax.erf(x * SQRT_HALF))


def _sigmoid(x):
    return 0.5 * jnp.tanh(0.5 * x) + 0.5


def _qkv_kernel(x_ref, g0_ref, b0_ref, w_ref, b_ref, wfg_ref, bfg_ref,
                h0_ref, qk_ref, v_ref, fg_ref):
    def ln(c):
        rows = slice(c * RCQ, (c + 1) * RCQ)
        hb = _ln_rows(x_ref[rows, :], g0_ref[...], b0_ref[...]).astype(BF16)
        h0_ref[rows, :] = hb
        return hb

    nchunk = TMQ // RCQ
    hb_next = ln(0)
    for c in range(nchunk):
        rows = slice(c * RCQ, (c + 1) * RCQ)
        hb = hb_next
        if c + 1 < nchunk:
            hb_next = ln(c + 1)
        acc = jnp.dot(hb, w_ref[...], preferred_element_type=F32) + b_ref[...]
        qk_ref[rows, :] = acc[:, :D_MODEL].astype(BF16)
        v_ref[rows, :] = acc[:, D_MODEL:].astype(BF16)
        fg = jnp.dot(hb, wfg_ref[...], preferred_element_type=F32) + bfg_ref[...]
        fg_ref[rows, :] = fg.astype(BF16)


def _qkv(x2, g0, b0, w, bias):
    n = x2.shape[0]
    fg_blk = FG_COL0 // FG_PAD
    const = lambda shape, idx: pl.BlockSpec(shape, lambda i: idx, pipeline_mode=pl.Buffered(1))
    tile = lambda width: pl.BlockSpec((TMQ, width), lambda i: (i, 0))
    return pl.pallas_call(
        _qkv_kernel,
        grid=(n // TMQ,),
        in_specs=[
            tile(D_MODEL),
            const((1, D_MODEL), (0, 0)),
            const((1, D_MODEL), (0, 0)),
            const((D_MODEL, 2 * D_MODEL), (0, 0)),
            const((1, 2 * D_MODEL), (0, 0)),
            const((D_MODEL, FG_PAD), (0, fg_blk)),
            const((1, FG_PAD), (0, fg_blk)),
        ],
        out_specs=[tile(D_MODEL), tile(D_MODEL), tile(D_MODEL), tile(FG_PAD)],
        out_shape=[
            jax.ShapeDtypeStruct((n, D_MODEL), BF16),
            jax.ShapeDtypeStruct((n, D_MODEL), BF16),
            jax.ShapeDtypeStruct((n, D_MODEL), BF16),
            jax.ShapeDtypeStruct((n, FG_PAD), BF16),
        ],
        compiler_params=pltpu.CompilerParams(
            dimension_semantics=("arbitrary",),
            vmem_limit_bytes=VMEM_LIMIT),
        name="ln0_qkv",
    )(x2, g0, b0, w, bias, w, bias)


def _proj_kernel(nsec, act, hb_ref, w_ref, b_ref, zg_ref, zb_ref, *rest):
    out_refs, tmp_ref = rest[:nsec], rest[-1]
    s = pl.program_id(0)

    def run(out_ref):
        for ci in range(TM1 // RC1):
            rows = slice(ci * RC1, (ci + 1) * RC1)
            acc = jnp.dot(hb_ref[rows, :], w_ref[...], preferred_element_type=F32) + b_ref[...]
            tmp_ref[ci % 2] = acc
            out_ref[rows, :] = act(tmp_ref[ci % 2], zg_ref, zb_ref).astype(BF16)

    if nsec == 1:
        run(out_refs[0])
    else:
        for k in range(nsec):
            pl.when(s == k)(functools.partial(run, out_refs[k]))


def _act_silu(a, zg_ref, zb_ref):
    return a * _sigmoid(a)


def _act_gelu(a, zg_ref, zb_ref):
    return _gelu(a)


def _act_sigmoid(a, zg_ref, zb_ref):
    return _sigmoid(a)


def _act_gelu_groupln(a, zg_ref, zb_ref):
    parts = []
    for g in range(GMLP_GROUPS):
        cs = slice(g * GMLP_DG, (g + 1) * GMLP_DG)
        parts.append(_ln_rows(_gelu(a[:, cs]), zg_ref[:, cs], zb_ref[:, cs]))
    return jnp.concatenate(parts, axis=-1)


def _proj(hb, w, bias, blk0, nsec, act, zg, zb, name):
    n = hb.shape[0]
    nt = n // TM1

    def out_idx(k):
        return lambda s, i: (jnp.where(s == k, i, jnp.where(s > k, nt - 1, 0)), 0)

    in_specs = [
        pl.BlockSpec((TM1, D_MODEL), lambda s, i: (i, 0)),
        pl.BlockSpec((D_MODEL, D_MODEL), lambda s, i: (0, blk0 + s)),
        pl.BlockSpec((1, D_MODEL), lambda s, i: (0, blk0 + s)),
        pl.BlockSpec((1, D_MODEL), lambda s, i: (0, 0)),
        pl.BlockSpec((1, D_MODEL), lambda s, i: (0, 0)),
    ]
    operands = [hb, w, bias, zg, zb]
    out_specs = [pl.BlockSpec((TM1, D_MODEL), out_idx(k)) for k in range(nsec)]
    out_shape = [jax.ShapeDtypeStruct((n, D_MODEL), BF16) for _ in range(nsec)]
    return pl.pallas_call(
        functools.partial(_proj_kernel, nsec, act),
        grid=(nsec, nt),
        in_specs=in_specs,
        out_specs=out_specs,
        out_shape=out_shape,
        scratch_shapes=[pltpu.VMEM((2, RC1, D_MODEL), F32)],
        compiler_params=pltpu.CompilerParams(
            dimension_semantics=("arbitrary", "arbitrary"),
            vmem_limit_bytes=VMEM_LIMIT),
        name=name,
    )(*operands)


def _mixer_kernel(qk_ref, v_ref, og_ref, u_ref, z_ref, ga_ref, gb_ref,
                  fg_ref, wf2_ref, bf2_ref, gn_ref, ws_ref, bs_ref,
                  m_ref, st_ref, yb_ref, kt_ref, dec_ref, stb_ref):
    @pl.when(pl.program_id(1) == 0)
    def _():
        st_ref[...] = jnp.zeros_like(st_ref)

    ri = lax.broadcasted_iota(jnp.int32, (GMLP_BLOCK, GMLP_BLOCK), 0) // CHUNK
    ci = lax.broadcasted_iota(jnp.int32, (GMLP_BLOCK, GMLP_BLOCK), 1) // CHUNK
    for g in range(GMLP_GROUPS):
        cs = slice(g * GMLP_DG, (g + 1) * GMLP_DG)
        wm = jnp.where(ri >= ci, ws_ref[g], 0.0).astype(BF16)
        for nb in range(TS2 // GMLP_BLOCK):
            rows = slice(nb * GMLP_BLOCK, (nb + 1) * GMLP_BLOCK)
            sg = jnp.dot(wm, z_ref[rows, cs], preferred_element_type=F32) + bs_ref[:, cs]
            yb_ref[rows, cs] = u_ref[rows, cs].astype(F32) * sg

    fl = jnp.dot(fg_ref[...], wf2_ref[...], preferred_element_type=F32) + bf2_ref[...]
    la = (jnp.minimum(fl, 0.0) - jnp.log1p(jnp.exp(-jnp.abs(fl)))) * (1.0 / GLA_TAU)

    tri = (lax.broadcasted_iota(jnp.int32, (CHUNK, CHUNK), 0)
           >= lax.broadcasted_iota(jnp.int32, (CHUNK, CHUNK), 1)).astype(BF16)

    heads = range(GLA_HEADS)
    kcs = [slice(h * GLA_DK, (h + 1) * GLA_DK) for h in heads]
    vcs = [slice(h * GLA_DV, (h + 1) * GLA_DV) for h in heads]

    for c in range(TS2 // CHUNK):
        rows = slice(c * CHUNK, (c + 1) * CHUNK)
        la_c = la[rows, :]
        p0 = la_c.astype(BF16)
        r0 = la_c - p0.astype(F32)
        p1 = r0.astype(BF16)
        p2 = (r0 - p1.astype(F32)).astype(BF16)
        b = (jnp.dot(tri, p0, preferred_element_type=F32)
             + jnp.dot(tri, p1, preferred_element_type=F32)
             + jnp.dot(tri, p2, preferred_element_type=F32))
        bend = b[CHUNK - 1:CHUNK, :]
        kt_ref[rows, :] = (qk_ref[rows, N_QK:].astype(F32) * jnp.exp(bend - b)).astype(BF16)
        dec_ref[c] = jnp.transpose(jnp.broadcast_to(jnp.exp(bend), (GMLP_BLOCK, N_QK)))

    for c in range(TS2 // CHUNK):
        rows = slice(c * CHUNK, (c + 1) * CHUNK)
        for h0 in range(0, GLA_HEADS, HEAD_GROUP):
            grp = range(h0, h0 + HEAD_GROUP)
            upd = {h: lax.dot_general(kt_ref[rows, kcs[h]], v_ref[rows, vcs[h]],
                                      (((0,), (0,)), ((), ())), preferred_element_type=F32)
                   for h in grp}
            for h in grp:
                dcol = jnp.concatenate([dec_ref[c, kcs[h], :]] * (GLA_DV // 128), axis=1)
                st = st_ref[h] * dcol + upd[h]
                st_ref[h] = st
                stb_ref[h] = st.astype(BF16)
            o = {h: jnp.dot(qk_ref[rows, kcs[h]], stb_ref[h], preferred_element_type=F32)
                 * (GLA_DK ** -0.5) for h in grp}
            for h in grp:
                vc = vcs[h]
                ms = jnp.mean(o[h] * o[h], axis=-1, keepdims=True)
                ya = (o[h] * lax.rsqrt(ms + LN_EPS) * gn_ref[:, vc]) * og_ref[rows, vc].astype(F32)
                mm = (ga_ref[rows, vc].astype(F32) * ya
                      + gb_ref[rows, vc].astype(F32) * yb_ref[rows, vc])
                m_ref[rows, vc] = mm.astype(BF16)


def _mixer(qk, v, og, u, z, ga, gb, fg, wf2, bf2, gn, ws, bs_full, bsz, seq):
    nt = seq // TS2
    tile = lambda b, s: (b * nt + s, 0)
    return pl.pallas_call(
        _mixer_kernel,
        grid=(bsz, nt),
        in_specs=[
            pl.BlockSpec((TS2, D_MODEL), tile),
            pl.BlockSpec((TS2, D_MODEL), tile),
            pl.BlockSpec((TS2, D_MODEL), tile),
            pl.BlockSpec((TS2, D_MODEL), tile),
            pl.BlockSpec((TS2, D_MODEL), tile),
            pl.BlockSpec((TS2, D_MODEL), tile),
            pl.BlockSpec((TS2, D_MODEL), tile),
            pl.BlockSpec((TS2, FG_PAD), tile),
            pl.BlockSpec((FG_PAD, N_QK), lambda b, s: (0, 0)),
            pl.BlockSpec((1, N_QK), lambda b, s: (0, 0)),
            pl.BlockSpec((1, D_MODEL), lambda b, s: (0, 0)),
            pl.BlockSpec((GMLP_GROUPS, GMLP_BLOCK, GMLP_BLOCK), lambda b, s: (0, 0, 0)),
            pl.BlockSpec((GMLP_BLOCK, D_MODEL), lambda b, s: (0, 0)),
        ],
        out_specs=pl.BlockSpec((TS2, D_MODEL), tile),
        out_shape=jax.ShapeDtypeStruct((bsz * seq, D_MODEL), BF16),
        scratch_shapes=[
            pltpu.VMEM((GLA_HEADS, GLA_DK, GLA_DV), F32),
            pltpu.VMEM((TS2, D_MODEL), F32),
            pltpu.VMEM((TS2, N_QK), BF16),
            pltpu.VMEM((TS2 // CHUNK, N_QK, GMLP_BLOCK), F32),
            pltpu.VMEM((GLA_HEADS, GLA_DK, GLA_DV), BF16),
        ],
        compiler_params=pltpu.CompilerParams(
            dimension_semantics=("arbitrary", "arbitrary"),
            vmem_limit_bytes=VMEM_LIMIT),
        name="mixer",
    )(qk, v, og, u, z, ga, gb, fg, wf2, bf2, gn, ws, bs_full)


def _outproj_kernel(m_ref, h0_ref, p_ref, wo_ref, wpg_ref, wpu_ref,
                    g1_ref, b1_ref, bpg_ref, h1_ref, ple_ref):
    def mix(c):
        rows = slice(c * RC3, (c + 1) * RC3)
        r = (ALPHA * h0_ref[rows, :].astype(F32)
             + jnp.dot(m_ref[rows, :], wo_ref[...], preferred_element_type=F32))
        hb = _ln_rows(r, g1_ref[...], b1_ref[...]).astype(BF16)
        h1_ref[rows, :] = hb
        return hb

    def embed(c, hb):
        rows = slice(c * RC3, (c + 1) * RC3)
        gate = _sigmoid(jnp.dot(hb, wpg_ref[...], preferred_element_type=F32) + bpg_ref[...])
        pu = jnp.dot(p_ref[rows, :].astype(BF16), wpu_ref[...], preferred_element_type=F32)
        ple_ref[rows, :] = (gate * pu).astype(BF16)

    nchunk = TM3 // RC3
    hb_next = mix(0)
    for c in range(nchunk):
        hb = hb_next
        if c + 1 < nchunk:
            hb_next = mix(c + 1)
        embed(c, hb)


def _outproj(m, h0, p2, wo, wpg, wpu, g1, b1, bpg):
    n = h0.shape[0]
    const = lambda shape: pl.BlockSpec(shape, lambda i: (0, 0), pipeline_mode=pl.Buffered(1))
    return pl.pallas_call(
        _outproj_kernel,
        grid=(n // TM3,),
        in_specs=[
            pl.BlockSpec((TM3, D_MODEL), lambda i: (i, 0)),
            pl.BlockSpec((TM3, D_MODEL), lambda i: (i, 0)),
            pl.BlockSpec((TM3, PLE_DIM), lambda i: (i, 0)),
            const((D_MODEL, D_MODEL)),
            const((D_MODEL, D_MODEL)),
            const((PLE_DIM, D_MODEL)),
            const((1, D_MODEL)), const((1, D_MODEL)), const((1, D_MODEL)),
        ],
        out_specs=[
            pl.BlockSpec((TM3, D_MODEL), lambda i: (i, 0)),
            pl.BlockSpec((TM3, D_MODEL), lambda i: (i, 0)),
        ],
        out_shape=[
            jax.ShapeDtypeStruct((n, D_MODEL), BF16),
            jax.ShapeDtypeStruct((n, D_MODEL), BF16),
        ],
        compiler_params=pltpu.CompilerParams(
            dimension_semantics=("arbitrary",),
            vmem_limit_bytes=VMEM_LIMIT),
        name="outproj",
    )(m, h0, p2, wo, wpg, wpu, g1, b1, bpg)


def _ffn_kernel(h1_ref, ple_ref, wg_ref, wu_ref, wd_ref, g2_ref, b2_ref, out_ref):
    f = pl.program_id(1)
    last = pl.num_programs(1) - 1

    def gate_up(r):
        hb = h1_ref[r * RC4:(r + 1) * RC4, :]
        gt = jnp.dot(hb, wg_ref[...], preferred_element_type=F32)
        up = jnp.dot(hb, wu_ref[...], preferred_element_type=F32)
        return (gt * _sigmoid(gt) * up).astype(BF16)

    def step(mode):
        nchunk = TM4 // RC4
        a_next = gate_up(0)
        for r in range(nchunk):
            rows = slice(r * RC4, (r + 1) * RC4)
            a = a_next
            if r + 1 < nchunk:
                a_next = gate_up(r + 1)
            d = jnp.dot(a, wd_ref[...], preferred_element_type=F32)
            if mode == "first":
                out_ref[rows, :] = d
            elif mode == "mid":
                out_ref[rows, :] += d
            else:
                res = (ALPHA * h1_ref[rows, :].astype(F32) + (out_ref[rows, :] + d)
                       + ple_ref[rows, :].astype(F32))
                out_ref[rows, :] = _ln_rows(res, g2_ref[...], b2_ref[...])

    pl.when(f == 0)(lambda: step("first"))
    pl.when((f > 0) & (f < last))(lambda: step("mid"))
    pl.when(f == last)(lambda: step("last"))


def _ffn(h1, ple, wgu, wd, g2, b2):
    n = h1.shape[0]
    nf = D_FF // TF4
    return pl.pallas_call(
        _ffn_kernel,
        grid=(n // TM4, nf),
        in_specs=[
            pl.BlockSpec((TM4, D_MODEL), lambda i, f: (i, 0)),
            pl.BlockSpec((TM4, D_MODEL), lambda i, f: (i, 0)),
            pl.BlockSpec((D_MODEL, TF4), lambda i, f: (0, f)),
            pl.BlockSpec((D_MODEL, TF4), lambda i, f: (0, f + nf)),
            pl.BlockSpec((TF4, D_MODEL), lambda i, f: (f, 0)),
            pl.BlockSpec((1, D_MODEL), lambda i, f: (0, 0)),
            pl.BlockSpec((1, D_MODEL), lambda i, f: (0, 0)),
        ],
        out_specs=pl.BlockSpec((TM4, D_MODEL), lambda i, f: (i, 0)),
        out_shape=jax.ShapeDtypeStruct((n, D_MODEL), F32),
        compiler_params=pltpu.CompilerParams(
            dimension_semantics=("arbitrary", "arbitrary"),
            vmem_limit_bytes=VMEM_LIMIT),
        name="ffn",
    )(h1, ple, wgu, wgu, wd, g2, b2)


def kernel(x, p, ln0_g, ln0_b, w_in, b_in, w_f2, b_f2, gla_norm_g, gmlp_ln_g, gmlp_ln_b,
           w_s, b_s, w_o, ln1_g, ln1_b, w_gu, w_down, w_pg, b_pg, w_pu, ln2_g, ln2_b):
    bsz, seq, d = x.shape
    assert d == D_MODEL and w_in.shape[0] == 1
    n = bsz * seq
    assert n % TM1 == 0 and seq % TS2 == 0 and n % TM4 == 0 and n % TM3 == 0 and n % TMQ == 0
    row = lambda a: a.reshape(1, -1).astype(F32)

    wi, bi = w_in[0], b_in[0]
    col_b = FG_COL0 + FG_WIDTH
    w_all = wi.astype(BF16)
    w_b = wi[:, col_b:].astype(BF16)
    b_all = bi.reshape(1, -1)
    b_b = bi[col_b:].reshape(1, -1)
    wf2 = jnp.zeros((FG_PAD, N_QK), F32)
    for h in range(GLA_HEADS):
        wf2 = wf2.at[h * GLA_RANK:(h + 1) * GLA_RANK, h * GLA_DK:(h + 1) * GLA_DK].set(w_f2[0, h])
    wf2 = wf2.astype(BF16)
    bs_full = jnp.repeat(b_s[0].T, GMLP_DG, axis=1)

    x2 = x.reshape(n, d)
    p2 = p[0].reshape(n, PLE_DIM)

    h0, qk, v, fg = _qkv(x2, row(ln0_g), row(ln0_b), w_all, b_all)
    zg, zb = row(gmlp_ln_g[0]), row(gmlp_ln_b[0])
    og, = _proj(h0, w_all, b_all, 2, 1, _act_silu, zg, zb, "proj_og")
    u, = _proj(h0, w_b, b_b, 0, 1, _act_gelu, zg, zb, "proj_u")
    z, = _proj(h0, w_b, b_b, 1, 1, _act_gelu_groupln, zg, zb, "proj_z")
    ga, gb = _proj(h0, w_b, b_b, 2, 2, _act_sigmoid, zg, zb, "proj_gates")
    m = _mixer(qk, v, og, u, z, ga, gb, fg, wf2, row(b_f2[0]), row(gla_norm_g[0]), w_s[0],
               bs_full, bsz, seq)
    h1, ple = _outproj(m, h0, p2, w_o[0].astype(BF16), w_pg[0].astype(BF16),
                       w_pu[0].astype(BF16), row(ln1_g[0]), row(ln1_b[0]), row(b_pg[0]))
    out = _ffn(h1, ple, w_gu[0].astype(BF16), w_down[0].astype(BF16),
               row(ln2_g[0]), row(ln2_b[0]))
    return out.reshape(bsz, seq, d)
```

```python
import functools

import jax
import jax.numpy as jnp
from jax import lax
from jax.experimental import pallas as pl
from jax.experimental.pallas import tpu as pltpu

F32 = jnp.float32
BF16 = jnp.bfloat16

D_MODEL = 2048
CHUNK = 64
GLA_HEADS = 4
GLA_DK = 256
GLA_DV = 512
GLA_RANK = 16
GLA_TAU = 16.0
GMLP_BLOCK = 128
GMLP_GROUPS = 4
GMLP_DG = 512
D_FF = 5632
PLE_DIM = 256
ALPHA = 2.0 ** 0.25
LN_EPS = 1e-5
SQRT_HALF = 0.7071067811865476

N_QK = GLA_HEADS * GLA_DK
FG_COL0 = 2 * N_QK + 2 * D_MODEL
FG_WIDTH = GLA_HEADS * GLA_RANK
FG_PAD = 128

VMEM_LIMIT = 56 * 1024 * 1024

TMQ, RCQ = 512, 256
TM1, RC1 = 1024, 256
TS2 = 512
HEAD_GROUP = 2
TM3, RC3 = 512, 256
TM4, TF4, RC4 = 1024, 512, 256


def _ln_rows(x, g, b):
    mu = jnp.mean(x, axis=-1, keepdims=True)
    xc = x - mu
    var = jnp.mean(xc * xc, axis=-1, keepdims=True)
    return xc * lax.rsqrt(var + LN_EPS) * g + b


def _gelu(x):
    return 0.5 * x * (1.0 + lax.erf(x * SQRT_HALF))


def _sigmoid(x):
    return 0.5 * jnp.tanh(0.5 * x) + 0.5


def _qkv_kernel(x_ref, g0_ref, b0_ref, w_ref, b_ref, wfg_ref, bfg_ref,
                h0_ref, qk_ref, v_ref, fg_ref):
    def ln(c):
        rows = slice(c * RCQ, (c + 1) * RCQ)
        hb = _ln_rows(x_ref[rows, :], g0_ref[...], b0_ref[...]).astype(BF16)
        h0_ref[rows, :] = hb
        return hb

    nchunk = TMQ // RCQ
    hb_next = ln(0)
    for c in range(nchunk):
        rows = slice(c * RCQ, (c + 1) * RCQ)
        hb = hb_next
        if c + 1 < nchunk:
            hb_next = ln(c + 1)
        acc = jnp.dot(hb, w_ref[...], preferred_element_type=F32) + b_ref[...]
        qk_ref[rows, :] = acc[:, :D_MODEL].astype(BF16)
        v_ref[rows, :] = acc[:, D_MODEL:].astype(BF16)
        fg = jnp.dot(hb, wfg_ref[...], preferred_element_type=F32) + bfg_ref[...]
        fg_ref[rows, :] = fg.astype(BF16)


def _qkv(x2, g0, b0, w, bias):
    n = x2.shape[0]
    fg_blk = FG_COL0 // FG_PAD
    const = lambda shape, idx: pl.BlockSpec(shape, lambda i: idx, pipeline_mode=pl.Buffered(1))
    tile = lambda width: pl.BlockSpec((TMQ, width), lambda i: (i, 0))
    return pl.pallas_call(
        _qkv_kernel,
        grid=(n // TMQ,),
        in_specs=[
            tile(D_MODEL),
            const((1, D_MODEL), (0, 0)),
            const((1, D_MODEL), (0, 0)),
            const((D_MODEL, 2 * D_MODEL), (0, 0)),
            const((1, 2 * D_MODEL), (0, 0)),
            const((D_MODEL, FG_PAD), (0, fg_blk)),
            const((1, FG_PAD), (0, fg_blk)),
        ],
        out_specs=[tile(D_MODEL), tile(D_MODEL), tile(D_MODEL), tile(FG_PAD)],
        out_shape=[
            jax.ShapeDtypeStruct((n, D_MODEL), BF16),
            jax.ShapeDtypeStruct((n, D_MODEL), BF16),
            jax.ShapeDtypeStruct((n, D_MODEL), BF16),
            jax.ShapeDtypeStruct((n, FG_PAD), BF16),
        ],
        compiler_params=pltpu.CompilerParams(
            dimension_semantics=("arbitrary",),
            vmem_limit_bytes=VMEM_LIMIT),
        name="ln0_qkv",
    )(x2, g0, b0, w, bias, w, bias)


def _proj_kernel(nsec, act, staged, hb_ref, w_ref, b_ref, zg_ref, zb_ref, *rest):
    out_refs, tmp_ref = rest[:nsec], rest[-1]
    s = pl.program_id(0)

    def run(out_ref):
        for ci in range(TM1 // RC1):
            rows = slice(ci * RC1, (ci + 1) * RC1)
            acc = jnp.dot(hb_ref[rows, :], w_ref[...], preferred_element_type=F32) + b_ref[...]
            if staged:
                tmp_ref[ci % 2] = acc
                acc = tmp_ref[ci % 2]
            out_ref[rows, :] = act(acc, zg_ref, zb_ref).astype(BF16)

    if nsec == 1:
        run(out_refs[0])
    else:
        for k in range(nsec):
            pl.when(s == k)(functools.partial(run, out_refs[k]))


def _act_silu(a, zg_ref, zb_ref):
    return a * _sigmoid(a)


def _act_gelu(a, zg_ref, zb_ref):
    return _gelu(a)


def _act_sigmoid(a, zg_ref, zb_ref):
    return _sigmoid(a)


def _act_gelu_groupln(a, zg_ref, zb_ref):
    parts = []
    for g in range(GMLP_GROUPS):
        cs = slice(g * GMLP_DG, (g + 1) * GMLP_DG)
        parts.append(_ln_rows(_gelu(a[:, cs]), zg_ref[:, cs], zb_ref[:, cs]))
    return jnp.concatenate(parts, axis=-1)


def _proj(hb, w, bias, blk0, nsec, act, zg, zb, name, staged=True):
    n = hb.shape[0]
    nt = n // TM1

    def out_idx(k):
        return lambda s, i: (jnp.where(s == k, i, jnp.where(s > k, nt - 1, 0)), 0)

    in_specs = [
        pl.BlockSpec((TM1, D_MODEL), lambda s, i: (i, 0)),
        pl.BlockSpec((D_MODEL, D_MODEL), lambda s, i: (0, blk0 + s)),
        pl.BlockSpec((1, D_MODEL), lambda s, i: (0, blk0 + s)),
        pl.BlockSpec((1, D_MODEL), lambda s, i: (0, 0)),
        pl.BlockSpec((1, D_MODEL), lambda s, i: (0, 0)),
    ]
    operands = [hb, w, bias, zg, zb]
    out_specs = [pl.BlockSpec((TM1, D_MODEL), out_idx(k)) for k in range(nsec)]
    out_shape = [jax.ShapeDtypeStruct((n, D_MODEL), BF16) for _ in range(nsec)]
    return pl.pallas_call(
        functools.partial(_proj_kernel, nsec, act, staged),
        grid=(nsec, nt),
        in_specs=in_specs,
        out_specs=out_specs,
        out_shape=out_shape,
        scratch_shapes=[pltpu.VMEM((2, RC1, D_MODEL), F32)],
        compiler_params=pltpu.CompilerParams(
            dimension_semantics=("arbitrary", "arbitrary"),
            vmem_limit_bytes=VMEM_LIMIT),
        name=name,
    )(*operands)


def _mixer_kernel(qk_ref, v_ref, og_ref, u_ref, z_ref, ga_ref, gb_ref,
                  fg_ref, wf2_ref, bf2_ref, gn_ref, ws_ref, bs_ref,
                  m_ref, st_ref, yb_ref, kt_ref, dec_ref, stb_ref):
    @pl.when(pl.program_id(1) == 0)
    def _():
        st_ref[...] = jnp.zeros_like(st_ref)

    ri = lax.broadcasted_iota(jnp.int32, (GMLP_BLOCK, GMLP_BLOCK), 0) // CHUNK
    ci = lax.broadcasted_iota(jnp.int32, (GMLP_BLOCK, GMLP_BLOCK), 1) // CHUNK
    for g in range(GMLP_GROUPS):
        cs = slice(g * GMLP_DG, (g + 1) * GMLP_DG)
        wm = jnp.where(ri >= ci, ws_ref[g], 0.0).astype(BF16)
        for nb in range(TS2 // GMLP_BLOCK):
            rows = slice(nb * GMLP_BLOCK, (nb + 1) * GMLP_BLOCK)
            sg = jnp.dot(wm, z_ref[rows, cs], preferred_element_type=F32) + bs_ref[:, cs]
            yb_ref[rows, cs] = u_ref[rows, cs].astype(F32) * sg

    fl = jnp.dot(fg_ref[...], wf2_ref[...], preferred_element_type=F32) + bf2_ref[...]
    la = (jnp.minimum(fl, 0.0) - jnp.log1p(jnp.exp(-jnp.abs(fl)))) * (1.0 / GLA_TAU)

    tri = (lax.broadcasted_iota(jnp.int32, (CHUNK, CHUNK), 0)
           >= lax.broadcasted_iota(jnp.int32, (CHUNK, CHUNK), 1)).astype(BF16)

    heads = range(GLA_HEADS)
    kcs = [slice(h * GLA_DK, (h + 1) * GLA_DK) for h in heads]
    vcs = [slice(h * GLA_DV, (h + 1) * GLA_DV) for h in heads]

    for c in range(TS2 // CHUNK):
        rows = slice(c * CHUNK, (c + 1) * CHUNK)
        la_c = la[rows, :]
        p0 = la_c.astype(BF16)
        r0 = la_c - p0.astype(F32)
        p1 = r0.astype(BF16)
        p2 = (r0 - p1.astype(F32)).astype(BF16)
        b = (jnp.dot(tri, p0, preferred_element_type=F32)
             + jnp.dot(tri, p1, preferred_element_type=F32)
             + jnp.dot(tri, p2, preferred_element_type=F32))
        bend = b[CHUNK - 1:CHUNK, :]
        kt_ref[rows, :] = (qk_ref[rows, N_QK:].astype(F32) * jnp.exp(bend - b)).astype(BF16)
        dec_ref[c] = jnp.transpose(jnp.broadcast_to(jnp.exp(bend), (GMLP_BLOCK, N_QK)))

    for c in range(TS2 // CHUNK):
        rows = slice(c * CHUNK, (c + 1) * CHUNK)
        for h0 in range(0, GLA_HEADS, HEAD_GROUP):
            grp = range(h0, h0 + HEAD_GROUP)
            upd = {h: lax.dot_general(kt_ref[rows, kcs[h]], v_ref[rows, vcs[h]],
                                      (((0,), (0,)), ((), ())), preferred_element_type=F32)
                   for h in grp}
            for h in grp:
                dcol = jnp.concatenate([dec_ref[c, kcs[h], :]] * (GLA_DV // 128), axis=1)
                st = st_ref[h] * dcol + upd[h]
                st_ref[h] = st
                stb_ref[h] = st.astype(BF16)
            o = {h: jnp.dot(qk_ref[rows, kcs[h]], stb_ref[h], preferred_element_type=F32)
                 * (GLA_DK ** -0.5) for h in grp}
            for h in grp:
                vc = vcs[h]
                ms = jnp.mean(o[h] * o[h], axis=-1, keepdims=True)
                ya = (o[h] * lax.rsqrt(ms + LN_EPS) * gn_ref[:, vc]) * og_ref[rows, vc].astype(F32)
                mm = (ga_ref[rows, vc].astype(F32) * ya
                      + gb_ref[rows, vc].astype(F32) * yb_ref[rows, vc])
                m_ref[rows, vc] = mm.astype(BF16)


def _mixer(qk, v, og, u, z, ga, gb, fg, wf2, bf2, gn, ws, bs_full, bsz, seq):
    nt = seq // TS2
    tile = lambda b, s: (b * nt + s, 0)
    return pl.pallas_call(
        _mixer_kernel,
        grid=(bsz, nt),
        in_specs=[
            pl.BlockSpec((TS2, D_MODEL), tile),
            pl.BlockSpec((TS2, D_MODEL), tile),
            pl.BlockSpec((TS2, D_MODEL), tile),
            pl.BlockSpec((TS2, D_MODEL), tile),
            pl.BlockSpec((TS2, D_MODEL), tile),
            pl.BlockSpec((TS2, D_MODEL), tile),
            pl.BlockSpec((TS2, D_MODEL), tile),
            pl.BlockSpec((TS2, FG_PAD), tile),
            pl.BlockSpec((FG_PAD, N_QK), lambda b, s: (0, 0)),
            pl.BlockSpec((1, N_QK), lambda b, s: (0, 0)),
            pl.BlockSpec((1, D_MODEL), lambda b, s: (0, 0)),
            pl.BlockSpec((GMLP_GROUPS, GMLP_BLOCK, GMLP_BLOCK), lambda b, s: (0, 0, 0)),
            pl.BlockSpec((GMLP_BLOCK, D_MODEL), lambda b, s: (0, 0)),
        ],
        out_specs=pl.BlockSpec((TS2, D_MODEL), tile),
        out_shape=jax.ShapeDtypeStruct((bsz * seq, D_MODEL), BF16),
        scratch_shapes=[
            pltpu.VMEM((GLA_HEADS, GLA_DK, GLA_DV), F32),
            pltpu.VMEM((TS2, D_MODEL), F32),
            pltpu.VMEM((TS2, N_QK), BF16),
            pltpu.VMEM((TS2 // CHUNK, N_QK, GMLP_BLOCK), F32),
            pltpu.VMEM((GLA_HEADS, GLA_DK, GLA_DV), BF16),
        ],
        compiler_params=pltpu.CompilerParams(
            dimension_semantics=("arbitrary", "arbitrary"),
            vmem_limit_bytes=VMEM_LIMIT),
        name="mixer",
    )(qk, v, og, u, z, ga, gb, fg, wf2, bf2, gn, ws, bs_full)


def _outproj_kernel(m_ref, h0_ref, p_ref, wo_ref, wpg_ref, wpu_ref,
                    g1_ref, b1_ref, bpg_ref, h1_ref, ple_ref):
    def mix(c):
        rows = slice(c * RC3, (c + 1) * RC3)
        r = (ALPHA * h0_ref[rows, :].astype(F32)
             + jnp.dot(m_ref[rows, :], wo_ref[...], preferred_element_type=F32))
        hb = _ln_rows(r, g1_ref[...], b1_ref[...]).astype(BF16)
        h1_ref[rows, :] = hb
        return hb

    def embed(c, hb):
        rows = slice(c * RC3, (c + 1) * RC3)
        gate = _sigmoid(jnp.dot(hb, wpg_ref[...], preferred_element_type=F32) + bpg_ref[...])
        pu = jnp.dot(p_ref[rows, :].astype(BF16), wpu_ref[...], preferred_element_type=F32)
        ple_ref[rows, :] = (gate * pu).astype(BF16)

    nchunk = TM3 // RC3
    hb_next = mix(0)
    for c in range(nchunk):
        hb = hb_next
        if c + 1 < nchunk:
            hb_next = mix(c + 1)
        embed(c, hb)


def _outproj(m, h0, p2, wo, wpg, wpu, g1, b1, bpg):
    n = h0.shape[0]
    const = lambda shape: pl.BlockSpec(shape, lambda i: (0, 0), pipeline_mode=pl.Buffered(1))
    return pl.pallas_call(
        _outproj_kernel,
        grid=(n // TM3,),
        in_specs=[
            pl.BlockSpec((TM3, D_MODEL), lambda i: (i, 0)),
            pl.BlockSpec((TM3, D_MODEL), lambda i: (i, 0)),
            pl.BlockSpec((TM3, PLE_DIM), lambda i: (i, 0)),
            const((D_MODEL, D_MODEL)),
            const((D_MODEL, D_MODEL)),
            const((PLE_DIM, D_MODEL)),
            const((1, D_MODEL)), const((1, D_MODEL)), const((1, D_MODEL)),
        ],
        out_specs=[
            pl.BlockSpec((TM3, D_MODEL), lambda i: (i, 0)),
            pl.BlockSpec((TM3, D_MODEL), lambda i: (i, 0)),
        ],
        out_shape=[
            jax.ShapeDtypeStruct((n, D_MODEL), BF16),
            jax.ShapeDtypeStruct((n, D_MODEL), BF16),
        ],
        compiler_params=pltpu.CompilerParams(
            dimension_semantics=("arbitrary",),
            vmem_limit_bytes=VMEM_LIMIT),
        name="outproj",
    )(m, h0, p2, wo, wpg, wpu, g1, b1, bpg)


def _ffn_kernel(h1_ref, ple_ref, wg_ref, wu_ref, wd_ref, g2_ref, b2_ref, out_ref):
    f = pl.program_id(1)
    last = pl.num_programs(1) - 1

    def gate_up(r):
        hb = h1_ref[r * RC4:(r + 1) * RC4, :]
        gt = jnp.dot(hb, wg_ref[...], preferred_element_type=F32)
        up = jnp.dot(hb, wu_ref[...], preferred_element_type=F32)
        return (gt * _sigmoid(gt) * up).astype(BF16)

    def step(mode):
        nchunk = TM4 // RC4
        a_next = gate_up(0)
        for r in range(nchunk):
            rows = slice(r * RC4, (r + 1) * RC4)
            a = a_next
            if r + 1 < nchunk:
                a_next = gate_up(r + 1)
            d = jnp.dot(a, wd_ref[...], preferred_element_type=F32)
            if mode == "first":
                out_ref[rows, :] = d
            elif mode == "mid":
                out_ref[rows, :] += d
            else:
                res = (ALPHA * h1_ref[rows, :].astype(F32) + (out_ref[rows, :] + d)
                       + ple_ref[rows, :].astype(F32))
                out_ref[rows, :] = _ln_rows(res, g2_ref[...], b2_ref[...])

    pl.when(f == 0)(lambda: step("first"))
    pl.when((f > 0) & (f < last))(lambda: step("mid"))
    pl.when(f == last)(lambda: step("last"))


def _ffn(h1, ple, wgu, wd, g2, b2):
    n = h1.shape[0]
    nf = D_FF // TF4
    return pl.pallas_call(
        _ffn_kernel,
        grid=(n // TM4, nf),
        in_specs=[
            pl.BlockSpec((TM4, D_MODEL), lambda i, f: (i, 0)),
            pl.BlockSpec((TM4, D_MODEL), lambda i, f: (i, 0)),
            pl.BlockSpec((D_MODEL, TF4), lambda i, f: (0, f)),
            pl.BlockSpec((D_MODEL, TF4), lambda i, f: (0, f + nf)),
            pl.BlockSpec((TF4, D_MODEL), lambda i, f: (f, 0)),
            pl.BlockSpec((1, D_MODEL), lambda i, f: (0, 0)),
            pl.BlockSpec((1, D_MODEL), lambda i, f: (0, 0)),
        ],
        out_specs=pl.BlockSpec((TM4, D_MODEL), lambda i, f: (i, 0)),
        out_shape=jax.ShapeDtypeStruct((n, D_MODEL), F32),
        compiler_params=pltpu.CompilerParams(
            dimension_semantics=("arbitrary", "arbitrary"),
            vmem_limit_bytes=VMEM_LIMIT),
        name="ffn",
    )(h1, ple, wgu, wgu, wd, g2, b2)


def kernel(x, p, ln0_g, ln0_b, w_in, b_in, w_f2, b_f2, gla_norm_g, gmlp_ln_g, gmlp_ln_b,
           w_s, b_s, w_o, ln1_g, ln1_b, w_gu, w_down, w_pg, b_pg, w_pu, ln2_g, ln2_b):
    bsz, seq, d = x.shape
    assert d == D_MODEL and w_in.shape[0] == 1
    n = bsz * seq
    assert n % TM1 == 0 and seq % TS2 == 0 and n % TM4 == 0 and n % TM3 == 0 and n % TMQ == 0
    row = lambda a: a.reshape(1, -1).astype(F32)

    wi, bi = w_in[0], b_in[0]
    col_b = FG_COL0 + FG_WIDTH
    w_all = wi.astype(BF16)
    w_b = wi[:, col_b:].astype(BF16)
    b_all = bi.reshape(1, -1)
    b_b = bi[col_b:].reshape(1, -1)
    wf2 = jnp.zeros((FG_PAD, N_QK), F32)
    for h in range(GLA_HEADS):
        wf2 = wf2.at[h * GLA_RANK:(h + 1) * GLA_RANK, h * GLA_DK:(h + 1) * GLA_DK].set(w_f2[0, h])
    wf2 = wf2.astype(BF16)
    bs_full = jnp.repeat(b_s[0].T, GMLP_DG, axis=1)

    x2 = x.reshape(n, d)
    p2 = p[0].reshape(n, PLE_DIM)

    h0, qk, v, fg = _qkv(x2, row(ln0_g), row(ln0_b), w_all, b_all)
    zg, zb = row(gmlp_ln_g[0]), row(gmlp_ln_b[0])
    og, = _proj(h0, w_all, b_all, 2, 1, _act_silu, zg, zb, "proj_og", staged=False)
    u, = _proj(h0, w_b, b_b, 0, 1, _act_gelu, zg, zb, "proj_u")
    z, = _proj(h0, w_b, b_b, 1, 1, _act_gelu_groupln, zg, zb, "proj_z")
    ga, gb = _proj(h0, w_b, b_b, 2, 2, _act_sigmoid, zg, zb, "proj_gates")
    m = _mixer(qk, v, og, u, z, ga, gb, fg, wf2, row(b_f2[0]), row(gla_norm_g[0]), w_s[0],
               bs_full, bsz, seq)
    h1, ple = _outproj(m, h0, p2, w_o[0].astype(BF16), w_pg[0].astype(BF16),
                       w_pu[0].astype(BF16), row(ln1_g[0]), row(ln1_b[0]), row(b_pg[0]))
    out = _ffn(h1, ple, w_gu[0].astype(BF16), w_down[0].astype(BF16),
               row(ln2_g[0]), row(ln2_b[0]))
    return out.reshape(bsz, seq, d)
```

```python
import jax
import jax.numpy as jnp
from jax import lax
from jax.experimental import pallas as pl
from jax.experimental.pallas import tpu as pltpu

F32 = jnp.float32
BF16 = jnp.bfloat16

D_MODEL = 2048
CHUNK = 64
GLA_HEADS = 4
GLA_DK = 256
GLA_DV = 512
GLA_RANK = 16
GLA_TAU = 16.0
GMLP_BLOCK = 128
GMLP_GROUPS = 4
GMLP_DG = 512
D_FF = 5632
PLE_DIM = 256
ALPHA = 2.0 ** 0.25
LN_EPS = 1e-5
SQRT_HALF = 0.7071067811865476

N_QK = GLA_HEADS * GLA_DK
FG_COL0 = 2 * N_QK + 2 * D_MODEL
FG_WIDTH = GLA_HEADS * GLA_RANK
FG_PAD = 128

VMEM_LIMIT = 56 * 1024 * 1024

TMQ, RCQ = 512, 256
TM1, RC1 = 512, 256
N_PROJ_SEC = 5
TS2 = 512
HEAD_GROUP = 2
TM3, RC3 = 512, 256
TM4, TF4, RC4 = 1024, 512, 256


def _ln_rows(x, g, b):
    mu = jnp.mean(x, axis=-1, keepdims=True)
    xc = x - mu
    var = jnp.mean(xc * xc, axis=-1, keepdims=True)
    return xc * lax.rsqrt(var + LN_EPS) * g + b


def _gelu(x):
    return 0.5 * x * (1.0 + lax.erf(x * SQRT_HALF))


def _sigmoid(x):
    return 0.5 * jnp.tanh(0.5 * x) + 0.5


def _qkv_kernel(x_ref, g0_ref, b0_ref, w_ref, b_ref, wfg_ref, bfg_ref,
                h0_ref, qk_ref, v_ref, fg_ref):
    def ln(c):
        rows = slice(c * RCQ, (c + 1) * RCQ)
        hb = _ln_rows(x_ref[rows, :], g0_ref[...], b0_ref[...]).astype(BF16)
        h0_ref[rows, :] = hb
        return hb

    nchunk = TMQ // RCQ
    hb_next = ln(0)
    for c in range(nchunk):
        rows = slice(c * RCQ, (c + 1) * RCQ)
        hb = hb_next
        if c + 1 < nchunk:
            hb_next = ln(c + 1)
        acc = jnp.dot(hb, w_ref[...], preferred_element_type=F32) + b_ref[...]
        qk_ref[rows, :] = acc[:, :D_MODEL].astype(BF16)
        v_ref[rows, :] = acc[:, D_MODEL:].astype(BF16)
        fg = jnp.dot(hb, wfg_ref[...], preferred_element_type=F32) + bfg_ref[...]
        fg_ref[rows, :] = fg.astype(BF16)


def _qkv(x2, g0, b0, w, bias):
    n = x2.shape[0]
    fg_blk = FG_COL0 // FG_PAD
    const = lambda shape, idx: pl.BlockSpec(shape, lambda i: idx, pipeline_mode=pl.Buffered(1))
    tile = lambda width: pl.BlockSpec((TMQ, width), lambda i: (i, 0))
    return pl.pallas_call(
        _qkv_kernel,
        grid=(n // TMQ,),
        in_specs=[
            tile(D_MODEL),
            const((1, D_MODEL), (0, 0)),
            const((1, D_MODEL), (0, 0)),
            const((D_MODEL, 2 * D_MODEL), (0, 0)),
            const((1, 2 * D_MODEL), (0, 0)),
            const((D_MODEL, FG_PAD), (0, fg_blk)),
            const((1, FG_PAD), (0, fg_blk)),
        ],
        out_specs=[tile(D_MODEL), tile(D_MODEL), tile(D_MODEL), tile(FG_PAD)],
        out_shape=[
            jax.ShapeDtypeStruct((n, D_MODEL), BF16),
            jax.ShapeDtypeStruct((n, D_MODEL), BF16),
            jax.ShapeDtypeStruct((n, D_MODEL), BF16),
            jax.ShapeDtypeStruct((n, FG_PAD), BF16),
        ],
        compiler_params=pltpu.CompilerParams(
            dimension_semantics=("arbitrary",),
            vmem_limit_bytes=VMEM_LIMIT),
        name="ln0_qkv",
    )(x2, g0, b0, w, bias, w, bias)


def _act_silu(a, zg_ref, zb_ref):
    return a * _sigmoid(a)


def _act_gelu(a, zg_ref, zb_ref):
    return _gelu(a)


def _act_sigmoid(a, zg_ref, zb_ref):
    return _sigmoid(a)


def _act_gelu_groupln(a, zg_ref, zb_ref):
    parts = []
    for g in range(GMLP_GROUPS):
        cs = slice(g * GMLP_DG, (g + 1) * GMLP_DG)
        parts.append(_ln_rows(_gelu(a[:, cs]), zg_ref[:, cs], zb_ref[:, cs]))
    return jnp.concatenate(parts, axis=-1)


def _proj_kernel(hb_ref, wa_ref, ba_ref, wb_ref, bb_ref, zg_ref, zb_ref, out_ref, tmp_ref):
    s = pl.program_id(0)

    def run(act, w_ref, b_ref):
        for ci in range(TM1 // RC1):
            rows = slice(ci * RC1, (ci + 1) * RC1)
            acc = jnp.dot(hb_ref[rows, :], w_ref[...], preferred_element_type=F32) + b_ref[...]
            tmp_ref[ci % 2] = acc
            out_ref[rows, :] = act(tmp_ref[ci % 2], zg_ref, zb_ref).astype(BF16)

    pl.when(s == 0)(lambda: run(_act_silu, wa_ref, ba_ref))
    pl.when(s == 1)(lambda: run(_act_gelu, wb_ref, bb_ref))
    pl.when(s == 2)(lambda: run(_act_gelu_groupln, wb_ref, bb_ref))
    pl.when(s >= 3)(lambda: run(_act_sigmoid, wb_ref, bb_ref))


def _proj(hb, w_all, b_all, w_b, b_b, zg, zb):
    n = hb.shape[0]
    og_blk = 2
    wb_idx = lambda s, i: (0, jnp.maximum(s - 1, 0))
    return pl.pallas_call(
        _proj_kernel,
        grid=(N_PROJ_SEC, n // TM1),
        in_specs=[
            pl.BlockSpec((TM1, D_MODEL), lambda s, i: (i, 0)),
            pl.BlockSpec((D_MODEL, D_MODEL), lambda s, i: (0, og_blk)),
            pl.BlockSpec((1, D_MODEL), lambda s, i: (0, og_blk)),
            pl.BlockSpec((D_MODEL, D_MODEL), wb_idx),
            pl.BlockSpec((1, D_MODEL), wb_idx),
            pl.BlockSpec((1, D_MODEL), lambda s, i: (0, 0)),
            pl.BlockSpec((1, D_MODEL), lambda s, i: (0, 0)),
        ],
        out_specs=pl.BlockSpec((TM1, D_MODEL), lambda s, i: (i, s)),
        out_shape=jax.ShapeDtypeStruct((n, N_PROJ_SEC * D_MODEL), BF16),
        scratch_shapes=[pltpu.VMEM((2, RC1, D_MODEL), F32)],
        compiler_params=pltpu.CompilerParams(
            dimension_semantics=("arbitrary", "arbitrary"),
            vmem_limit_bytes=VMEM_LIMIT),
        name="proj",
    )(hb, w_all, b_all, w_b, b_b, zg, zb)


def _mixer_kernel(qk_ref, v_ref, og_ref, u_ref, z_ref, ga_ref, gb_ref,
                  fg_ref, wf2_ref, bf2_ref, gn_ref, ws_ref, bs_ref,
                  m_ref, st_ref, yb_ref, kt_ref, dec_ref, stb_ref):
    @pl.when(pl.program_id(1) == 0)
    def _():
        st_ref[...] = jnp.zeros_like(st_ref)

    ri = lax.broadcasted_iota(jnp.int32, (GMLP_BLOCK, GMLP_BLOCK), 0) // CHUNK
    ci = lax.broadcasted_iota(jnp.int32, (GMLP_BLOCK, GMLP_BLOCK), 1) // CHUNK
    for g in range(GMLP_GROUPS):
        cs = slice(g * GMLP_DG, (g + 1) * GMLP_DG)
        wm = jnp.where(ri >= ci, ws_ref[g], 0.0).astype(BF16)
        for nb in range(TS2 // GMLP_BLOCK):
            rows = slice(nb * GMLP_BLOCK, (nb + 1) * GMLP_BLOCK)
            sg = jnp.dot(wm, z_ref[rows, cs], preferred_element_type=F32) + bs_ref[:, cs]
            yb_ref[rows, cs] = u_ref[rows, cs].astype(F32) * sg

    fl = jnp.dot(fg_ref[...], wf2_ref[...], preferred_element_type=F32) + bf2_ref[...]
    la = (jnp.minimum(fl, 0.0) - jnp.log1p(jnp.exp(-jnp.abs(fl)))) * (1.0 / GLA_TAU)

    tri = (lax.broadcasted_iota(jnp.int32, (CHUNK, CHUNK), 0)
           >= lax.broadcasted_iota(jnp.int32, (CHUNK, CHUNK), 1)).astype(BF16)

    heads = range(GLA_HEADS)
    kcs = [slice(h * GLA_DK, (h + 1) * GLA_DK) for h in heads]
    vcs = [slice(h * GLA_DV, (h + 1) * GLA_DV) for h in heads]

    for c in range(TS2 // CHUNK):
        rows = slice(c * CHUNK, (c + 1) * CHUNK)
        la_c = la[rows, :]
        p0 = la_c.astype(BF16)
        r0 = la_c - p0.astype(F32)
        p1 = r0.astype(BF16)
        p2 = (r0 - p1.astype(F32)).astype(BF16)
        b = (jnp.dot(tri, p0, preferred_element_type=F32)
             + jnp.dot(tri, p1, preferred_element_type=F32)
             + jnp.dot(tri, p2, preferred_element_type=F32))
        bend = b[CHUNK - 1:CHUNK, :]
        kt_ref[rows, :] = (qk_ref[rows, N_QK:].astype(F32) * jnp.exp(bend - b)).astype(BF16)
        dec_ref[c] = jnp.transpose(jnp.broadcast_to(jnp.exp(bend), (GMLP_BLOCK, N_QK)))

    for c in range(TS2 // CHUNK):
        rows = slice(c * CHUNK, (c + 1) * CHUNK)
        for h0 in range(0, GLA_HEADS, HEAD_GROUP):
            grp = range(h0, h0 + HEAD_GROUP)
            upd = {h: lax.dot_general(kt_ref[rows, kcs[h]], v_ref[rows, vcs[h]],
                                      (((0,), (0,)), ((), ())), preferred_element_type=F32)
                   for h in grp}
            for h in grp:
                dcol = jnp.concatenate([dec_ref[c, kcs[h], :]] * (GLA_DV // 128), axis=1)
                st = st_ref[h] * dcol + upd[h]
                st_ref[h] = st
                stb_ref[h] = st.astype(BF16)
            o = {h: jnp.dot(qk_ref[rows, kcs[h]], stb_ref[h], preferred_element_type=F32)
                 * (GLA_DK ** -0.5) for h in grp}
            for h in grp:
                vc = vcs[h]
                ms = jnp.mean(o[h] * o[h], axis=-1, keepdims=True)
                ya = (o[h] * lax.rsqrt(ms + LN_EPS) * gn_ref[:, vc]) * og_ref[rows, vc].astype(F32)
                mm = (ga_ref[rows, vc].astype(F32) * ya
                      + gb_ref[rows, vc].astype(F32) * yb_ref[rows, vc])
                m_ref[rows, vc] = mm.astype(BF16)


def _mixer(qk, v, proj, fg, wf2, bf2, gn, ws, bs_full, bsz, seq):
    nt = seq // TS2
    tile = lambda b, s: (b * nt + s, 0)
    sec = lambda c: (lambda b, s: (b * nt + s, c))
    return pl.pallas_call(
        _mixer_kernel,
        grid=(bsz, nt),
        in_specs=[
            pl.BlockSpec((TS2, D_MODEL), tile),
            pl.BlockSpec((TS2, D_MODEL), tile),
            pl.BlockSpec((TS2, D_MODEL), sec(0)),
            pl.BlockSpec((TS2, D_MODEL), sec(1)),
            pl.BlockSpec((TS2, D_MODEL), sec(2)),
            pl.BlockSpec((TS2, D_MODEL), sec(3)),
            pl.BlockSpec((TS2, D_MODEL), sec(4)),
            pl.BlockSpec((TS2, FG_PAD), tile),
            pl.BlockSpec((FG_PAD, N_QK), lambda b, s: (0, 0)),
            pl.BlockSpec((1, N_QK), lambda b, s: (0, 0)),
            pl.BlockSpec((1, D_MODEL), lambda b, s: (0, 0)),
            pl.BlockSpec((GMLP_GROUPS, GMLP_BLOCK, GMLP_BLOCK), lambda b, s: (0, 0, 0)),
            pl.BlockSpec((GMLP_BLOCK, D_MODEL), lambda b, s: (0, 0)),
        ],
        out_specs=pl.BlockSpec((TS2, D_MODEL), tile),
        out_shape=jax.ShapeDtypeStruct((bsz * seq, D_MODEL), BF16),
        scratch_shapes=[
            pltpu.VMEM((GLA_HEADS, GLA_DK, GLA_DV), F32),
            pltpu.VMEM((TS2, D_MODEL), F32),
            pltpu.VMEM((TS2, N_QK), BF16),
            pltpu.VMEM((TS2 // CHUNK, N_QK, GMLP_BLOCK), F32),
            pltpu.VMEM((GLA_HEADS, GLA_DK, GLA_DV), BF16),
        ],
        compiler_params=pltpu.CompilerParams(
            dimension_semantics=("arbitrary", "arbitrary"),
            vmem_limit_bytes=VMEM_LIMIT),
        name="mixer",
    )(qk, v, proj, proj, proj, proj, proj, fg, wf2, bf2, gn, ws, bs_full)


def _outproj_kernel(m_ref, h0_ref, p_ref, wo_ref, wpg_ref, wpu_ref,
                    g1_ref, b1_ref, bpg_ref, h1_ref, ple_ref):
    def mix(c):
        rows = slice(c * RC3, (c + 1) * RC3)
        r = (ALPHA * h0_ref[rows, :].astype(F32)
             + jnp.dot(m_ref[rows, :], wo_ref[...], preferred_element_type=F32))
        hb = _ln_rows(r, g1_ref[...], b1_ref[...]).astype(BF16)
        h1_ref[rows, :] = hb
        return hb

    def embed(c, hb):
        rows = slice(c * RC3, (c + 1) * RC3)
        gate = _sigmoid(jnp.dot(hb, wpg_ref[...], preferred_element_type=F32) + bpg_ref[...])
        pu = jnp.dot(p_ref[rows, :].astype(BF16), wpu_ref[...], preferred_element_type=F32)
        ple_ref[rows, :] = (gate * pu).astype(BF16)

    nchunk = TM3 // RC3
    hb_next = mix(0)
    for c in range(nchunk):
        hb = hb_next
        if c + 1 < nchunk:
            hb_next = mix(c + 1)
        embed(c, hb)


def _outproj(m, h0, p2, wo, wpg, wpu, g1, b1, bpg):
    n = h0.shape[0]
    const = lambda shape: pl.BlockSpec(shape, lambda i: (0, 0), pipeline_mode=pl.Buffered(1))
    return pl.pallas_call(
        _outproj_kernel,
        grid=(n // TM3,),
        in_specs=[
            pl.BlockSpec((TM3, D_MODEL), lambda i: (i, 0)),
            pl.BlockSpec((TM3, D_MODEL), lambda i: (i, 0)),
            pl.BlockSpec((TM3, PLE_DIM), lambda i: (i, 0)),
            const((D_MODEL, D_MODEL)),
            const((D_MODEL, D_MODEL)),
            const((PLE_DIM, D_MODEL)),
            const((1, D_MODEL)), const((1, D_MODEL)), const((1, D_MODEL)),
        ],
        out_specs=[
            pl.BlockSpec((TM3, D_MODEL), lambda i: (i, 0)),
            pl.BlockSpec((TM3, D_MODEL), lambda i: (i, 0)),
        ],
        out_shape=[
            jax.ShapeDtypeStruct((n, D_MODEL), BF16),
            jax.ShapeDtypeStruct((n, D_MODEL), BF16),
        ],
        compiler_params=pltpu.CompilerParams(
            dimension_semantics=("arbitrary",),
            vmem_limit_bytes=VMEM_LIMIT),
        name="outproj",
    )(m, h0, p2, wo, wpg, wpu, g1, b1, bpg)


def _ffn_kernel(h1_ref, ple_ref, wg_ref, wu_ref, wd_ref, g2_ref, b2_ref, out_ref):
    f = pl.program_id(1)
    last = pl.num_programs(1) - 1

    def gate_up(r):
        hb = h1_ref[r * RC4:(r + 1) * RC4, :]
        gt = jnp.dot(hb, wg_ref[...], preferred_element_type=F32)
        up = jnp.dot(hb, wu_ref[...], preferred_element_type=F32)
        return (gt * _sigmoid(gt) * up).astype(BF16)

    def step(mode):
        nchunk = TM4 // RC4
        a_next = gate_up(0)
        for r in range(nchunk):
            rows = slice(r * RC4, (r + 1) * RC4)
            a = a_next
            if r + 1 < nchunk:
                a_next = gate_up(r + 1)
            d = jnp.dot(a, wd_ref[...], preferred_element_type=F32)
            if mode == "first":
                out_ref[rows, :] = d
            elif mode == "mid":
                out_ref[rows, :] += d
            else:
                res = (ALPHA * h1_ref[rows, :].astype(F32) + (out_ref[rows, :] + d)
                       + ple_ref[rows, :].astype(F32))
                out_ref[rows, :] = _ln_rows(res, g2_ref[...], b2_ref[...])

    pl.when(f == 0)(lambda: step("first"))
    pl.when((f > 0) & (f < last))(lambda: step("mid"))
    pl.when(f == last)(lambda: step("last"))


def _ffn(h1, ple, wgu, wd, g2, b2):
    n = h1.shape[0]
    nf = D_FF // TF4
    return pl.pallas_call(
        _ffn_kernel,
        grid=(n // TM4, nf),
        in_specs=[
            pl.BlockSpec((TM4, D_MODEL), lambda i, f: (i, 0)),
            pl.BlockSpec((TM4, D_MODEL), lambda i, f: (i, 0)),
            pl.BlockSpec((D_MODEL, TF4), lambda i, f: (0, f)),
            pl.BlockSpec((D_MODEL, TF4), lambda i, f: (0, f + nf)),
            pl.BlockSpec((TF4, D_MODEL), lambda i, f: (f, 0)),
            pl.BlockSpec((1, D_MODEL), lambda i, f: (0, 0)),
            pl.BlockSpec((1, D_MODEL), lambda i, f: (0, 0)),
        ],
        out_specs=pl.BlockSpec((TM4, D_MODEL), lambda i, f: (i, 0)),
        out_shape=jax.ShapeDtypeStruct((n, D_MODEL), F32),
        compiler_params=pltpu.CompilerParams(
            dimension_semantics=("arbitrary", "arbitrary"),
            vmem_limit_bytes=VMEM_LIMIT),
        name="ffn",
    )(h1, ple, wgu, wgu, wd, g2, b2)


def kernel(x, p, ln0_g, ln0_b, w_in, b_in, w_f2, b_f2, gla_norm_g, gmlp_ln_g, gmlp_ln_b,
           w_s, b_s, w_o, ln1_g, ln1_b, w_gu, w_down, w_pg, b_pg, w_pu, ln2_g, ln2_b):
    bsz, seq, d = x.shape
    assert d == D_MODEL and w_in.shape[0] == 1
    n = bsz * seq
    assert n % TM1 == 0 and seq % TS2 == 0 and n % TM4 == 0 and n % TM3 == 0 and n % TMQ == 0
    row = lambda a: a.reshape(1, -1).astype(F32)

    wi, bi = w_in[0], b_in[0]
    col_b = FG_COL0 + FG_WIDTH
    w_all = wi.astype(BF16)
    w_b = wi[:, col_b:].astype(BF16)
    b_all = bi.reshape(1, -1)
    b_b = bi[col_b:].reshape(1, -1)
    wf2 = jnp.zeros((FG_PAD, N_QK), F32)
    for h in range(GLA_HEADS):
        wf2 = wf2.at[h * GLA_RANK:(h + 1) * GLA_RANK, h * GLA_DK:(h + 1) * GLA_DK].set(w_f2[0, h])
    wf2 = wf2.astype(BF16)
    bs_full = jnp.repeat(b_s[0].T, GMLP_DG, axis=1)

    x2 = x.reshape(n, d)
    p2 = p[0].reshape(n, PLE_DIM)

    h0, qk, v, fg = _qkv(x2, row(ln0_g), row(ln0_b), w_all, b_all)
    zg, zb = row(gmlp_ln_g[0]), row(gmlp_ln_b[0])
    proj = _proj(h0, w_all, b_all, w_b, b_b, zg, zb)
    m = _mixer(qk, v, proj, fg, wf2, row(b_f2[0]), row(gla_norm_g[0]), w_s[0], bs_full, bsz, seq)
    h1, ple = _outproj(m, h0, p2, w_o[0].astype(BF16), w_pg[0].astype(BF16),
                       w_pu[0].astype(BF16), row(ln1_g[0]), row(ln1_b[0]), row(b_pg[0]))
    out = _ffn(h1, ple, w_gu[0].astype(BF16), w_down[0].astype(BF16),
               row(ln2_g[0]), row(ln2_b[0]))
    return out.reshape(bsz, seq, d)
```

```python
import functools

import jax
import jax.numpy as jnp
from jax import lax
from jax.experimental import pallas as pl
from jax.experimental.pallas import tpu as pltpu

F32 = jnp.float32
BF16 = jnp.bfloat16

D_MODEL = 2048
CHUNK = 64
GLA_HEADS = 4
GLA_DK = 256
GLA_DV = 512
GLA_RANK = 16
GLA_TAU = 16.0
GMLP_BLOCK = 128
GMLP_GROUPS = 4
GMLP_DG = 512
D_FF = 5632
PLE_DIM = 256
ALPHA = 2.0 ** 0.25
LN_EPS = 1e-5
SQRT_HALF = 0.7071067811865476

N_QK = GLA_HEADS * GLA_DK
FG_COL0 = 2 * N_QK + 2 * D_MODEL
FG_WIDTH = GLA_HEADS * GLA_RANK
FG_PAD = 128

VMEM_LIMIT = 56 * 1024 * 1024

TMQ, RCQ = 512, 256
TM1, RC1 = 1024, 256
TS2 = 512
HEAD_GROUP = 2
TM3, RC3 = 512, 256
TM4, TF4, RC4 = 1024, 512, 256


def _ln_rows(x, g, b):
    mu = jnp.mean(x, axis=-1, keepdims=True)
    xc = x - mu
    var = jnp.mean(xc * xc, axis=-1, keepdims=True)
    return xc * lax.rsqrt(var + LN_EPS) * g + b


def _gelu(x):
    return 0.5 * x * (1.0 + lax.erf(x * SQRT_HALF))


def _sigmoid(x):
    return 0.5 * jnp.tanh(0.5 * x) + 0.5


def _qkv_kernel(x_ref, g0_ref, b0_ref, w_ref, b_ref, wfg_ref, bfg_ref,
                h0_ref, qk_ref, v_ref, fg_ref):
    def ln(c):
        rows = slice(c * RCQ, (c + 1) * RCQ)
        hb = _ln_rows(x_ref[rows, :], g0_ref[...], b0_ref[...]).astype(BF16)
        h0_ref[rows, :] = hb
        return hb

    nchunk = TMQ // RCQ
    hb_next = ln(0)
    for c in range(nchunk):
        rows = slice(c * RCQ, (c + 1) * RCQ)
        hb = hb_next
        if c + 1 < nchunk:
            hb_next = ln(c + 1)
        acc = jnp.dot(hb, w_ref[...], preferred_element_type=F32) + b_ref[...]
        qk_ref[rows, :] = acc[:, :D_MODEL].astype(BF16)
        v_ref[rows, :] = acc[:, D_MODEL:].astype(BF16)
        fg = jnp.dot(hb, wfg_ref[...], preferred_element_type=F32) + bfg_ref[...]
        fg_ref[rows, :] = fg.astype(BF16)


def _qkv(x2, g0, b0, w, bias):
    n = x2.shape[0]
    fg_blk = FG_COL0 // FG_PAD
    const = lambda shape, idx: pl.BlockSpec(shape, lambda i: idx, pipeline_mode=pl.Buffered(1))
    tile = lambda width: pl.BlockSpec((TMQ, width), lambda i: (i, 0))
    return pl.pallas_call(
        _qkv_kernel,
        grid=(n // TMQ,),
        in_specs=[
            tile(D_MODEL),
            const((1, D_MODEL), (0, 0)),
            const((1, D_MODEL), (0, 0)),
            const((D_MODEL, 2 * D_MODEL), (0, 0)),
            const((1, 2 * D_MODEL), (0, 0)),
            const((D_MODEL, FG_PAD), (0, fg_blk)),
            const((1, FG_PAD), (0, fg_blk)),
        ],
        out_specs=[tile(D_MODEL), tile(D_MODEL), tile(D_MODEL), tile(FG_PAD)],
        out_shape=[
            jax.ShapeDtypeStruct((n, D_MODEL), BF16),
            jax.ShapeDtypeStruct((n, D_MODEL), BF16),
            jax.ShapeDtypeStruct((n, D_MODEL), BF16),
            jax.ShapeDtypeStruct((n, FG_PAD), BF16),
        ],
        compiler_params=pltpu.CompilerParams(
            dimension_semantics=("arbitrary",),
            vmem_limit_bytes=VMEM_LIMIT),
        name="ln0_qkv",
    )(x2, g0, b0, w, bias, w, bias)


def _proj_kernel(nsec, act, hb_ref, w_ref, b_ref, zg_ref, zb_ref, *out_refs):
    s = pl.program_id(0)

    def run(out_ref):
        for ci in range(TM1 // RC1):
            rows = slice(ci * RC1, (ci + 1) * RC1)
            acc = jnp.dot(hb_ref[rows, :], w_ref[...], preferred_element_type=F32) + b_ref[...]
            out_ref[rows, :] = act(acc, zg_ref, zb_ref).astype(BF16)

    if nsec == 1:
        run(out_refs[0])
    else:
        for k in range(nsec):
            pl.when(s == k)(functools.partial(run, out_refs[k]))


def _act_silu(a, zg_ref, zb_ref):
    return a * _sigmoid(a)


def _act_gelu(a, zg_ref, zb_ref):
    return _gelu(a)


def _act_sigmoid(a, zg_ref, zb_ref):
    return _sigmoid(a)


def _act_gelu_groupln(a, zg_ref, zb_ref):
    parts = []
    for g in range(GMLP_GROUPS):
        cs = slice(g * GMLP_DG, (g + 1) * GMLP_DG)
        parts.append(_ln_rows(_gelu(a[:, cs]), zg_ref[:, cs], zb_ref[:, cs]))
    return jnp.concatenate(parts, axis=-1)


def _proj(hb, w, bias, blk0, nsec, act, zg, zb, name):
    n = hb.shape[0]
    nt = n // TM1

    def out_idx(k):
        return lambda s, i: (jnp.where(s == k, i, jnp.where(s > k, nt - 1, 0)), 0)

    in_specs = [
        pl.BlockSpec((TM1, D_MODEL), lambda s, i: (i, 0)),
        pl.BlockSpec((D_MODEL, D_MODEL), lambda s, i: (0, blk0 + s)),
        pl.BlockSpec((1, D_MODEL), lambda s, i: (0, blk0 + s)),
        pl.BlockSpec((1, D_MODEL), lambda s, i: (0, 0)),
        pl.BlockSpec((1, D_MODEL), lambda s, i: (0, 0)),
    ]
    operands = [hb, w, bias, zg, zb]
    out_specs = [pl.BlockSpec((TM1, D_MODEL), out_idx(k)) for k in range(nsec)]
    out_shape = [jax.ShapeDtypeStruct((n, D_MODEL), BF16) for _ in range(nsec)]
    return pl.pallas_call(
        functools.partial(_proj_kernel, nsec, act),
        grid=(nsec, nt),
        in_specs=in_specs,
        out_specs=out_specs,
        out_shape=out_shape,
        compiler_params=pltpu.CompilerParams(
            dimension_semantics=("arbitrary", "arbitrary"),
            vmem_limit_bytes=VMEM_LIMIT),
        name=name,
    )(*operands)


def _mixer_kernel(qk_ref, v_ref, og_ref, u_ref, z_ref, ga_ref, gb_ref,
                  fg_ref, wf2_ref, bf2_ref, gn_ref, ws_ref, bs_ref,
                  m_ref, st_ref, yb_ref, kt_ref, dec_ref, stb_ref):
    @pl.when(pl.program_id(1) == 0)
    def _():
        st_ref[...] = jnp.zeros_like(st_ref)

    ri = lax.broadcasted_iota(jnp.int32, (GMLP_BLOCK, GMLP_BLOCK), 0) // CHUNK
    ci = lax.broadcasted_iota(jnp.int32, (GMLP_BLOCK, GMLP_BLOCK), 1) // CHUNK
    for g in range(GMLP_GROUPS):
        cs = slice(g * GMLP_DG, (g + 1) * GMLP_DG)
        wm = jnp.where(ri >= ci, ws_ref[g], 0.0).astype(BF16)
        for nb in range(TS2 // GMLP_BLOCK):
            rows = slice(nb * GMLP_BLOCK, (nb + 1) * GMLP_BLOCK)
            sg = jnp.dot(wm, z_ref[rows, cs], preferred_element_type=F32) + bs_ref[:, cs]
            yb_ref[rows, cs] = u_ref[rows, cs].astype(F32) * sg

    fl = jnp.dot(fg_ref[...], wf2_ref[...], preferred_element_type=F32) + bf2_ref[...]
    la = (jnp.minimum(fl, 0.0) - jnp.log1p(jnp.exp(-jnp.abs(fl)))) * (1.0 / GLA_TAU)

    tri = (lax.broadcasted_iota(jnp.int32, (CHUNK, CHUNK), 0)
           >= lax.broadcasted_iota(jnp.int32, (CHUNK, CHUNK), 1)).astype(BF16)

    heads = range(GLA_HEADS)
    kcs = [slice(h * GLA_DK, (h + 1) * GLA_DK) for h in heads]
    vcs = [slice(h * GLA_DV, (h + 1) * GLA_DV) for h in heads]

    for c in range(TS2 // CHUNK):
        rows = slice(c * CHUNK, (c + 1) * CHUNK)
        la_c = la[rows, :]
        p0 = la_c.astype(BF16)
        r0 = la_c - p0.astype(F32)
        p1 = r0.astype(BF16)
        p2 = (r0 - p1.astype(F32)).astype(BF16)
        b = (jnp.dot(tri, p0, preferred_element_type=F32)
             + jnp.dot(tri, p1, preferred_element_type=F32)
             + jnp.dot(tri, p2, preferred_element_type=F32))
        bend = b[CHUNK - 1:CHUNK, :]
        kt_ref[rows, :] = (qk_ref[rows, N_QK:].astype(F32) * jnp.exp(bend - b)).astype(BF16)
        dec_ref[c] = jnp.transpose(jnp.broadcast_to(jnp.exp(bend), (GMLP_BLOCK, N_QK)))

    for c in range(TS2 // CHUNK):
        rows = slice(c * CHUNK, (c + 1) * CHUNK)
        for h0 in range(0, GLA_HEADS, HEAD_GROUP):
            grp = range(h0, h0 + HEAD_GROUP)
            upd = {h: lax.dot_general(kt_ref[rows, kcs[h]], v_ref[rows, vcs[h]],
                                      (((0,), (0,)), ((), ())), preferred_element_type=F32)
                   for h in grp}
            for h in grp:
                dcol = jnp.concatenate([dec_ref[c, kcs[h], :]] * (GLA_DV // 128), axis=1)
                st = st_ref[h] * dcol + upd[h]
                st_ref[h] = st
                stb_ref[h] = st.astype(BF16)
            o = {h: jnp.dot(qk_ref[rows, kcs[h]], stb_ref[h], preferred_element_type=F32)
                 * (GLA_DK ** -0.5) for h in grp}
            for h in grp:
                vc = vcs[h]
                ms = jnp.mean(o[h] * o[h], axis=-1, keepdims=True)
                ya = (o[h] * lax.rsqrt(ms + LN_EPS) * gn_ref[:, vc]) * og_ref[rows, vc].astype(F32)
                mm = (ga_ref[rows, vc].astype(F32) * ya
                      + gb_ref[rows, vc].astype(F32) * yb_ref[rows, vc])
                m_ref[rows, vc] = mm.astype(BF16)


def _mixer(qk, v, og, u, z, ga, gb, fg, wf2, bf2, gn, ws, bs_full, bsz, seq):
    nt = seq // TS2
    tile = lambda b, s: (b * nt + s, 0)
    return pl.pallas_call(
        _mixer_kernel,
        grid=(bsz, nt),
        in_specs=[
            pl.BlockSpec((TS2, D_MODEL), tile),
            pl.BlockSpec((TS2, D_MODEL), tile),
            pl.BlockSpec((TS2, D_MODEL), tile),
            pl.BlockSpec((TS2, D_MODEL), tile),
            pl.BlockSpec((TS2, D_MODEL), tile),
            pl.BlockSpec((TS2, D_MODEL), tile),
            pl.BlockSpec((TS2, D_MODEL), tile),
            pl.BlockSpec((TS2, FG_PAD), tile),
            pl.BlockSpec((FG_PAD, N_QK), lambda b, s: (0, 0)),
            pl.BlockSpec((1, N_QK), lambda b, s: (0, 0)),
            pl.BlockSpec((1, D_MODEL), lambda b, s: (0, 0)),
            pl.BlockSpec((GMLP_GROUPS, GMLP_BLOCK, GMLP_BLOCK), lambda b, s: (0, 0, 0)),
            pl.BlockSpec((GMLP_BLOCK, D_MODEL), lambda b, s: (0, 0)),
        ],
        out_specs=pl.BlockSpec((TS2, D_MODEL), tile),
        out_shape=jax.ShapeDtypeStruct((bsz * seq, D_MODEL), BF16),
        scratch_shapes=[
            pltpu.VMEM((GLA_HEADS, GLA_DK, GLA_DV), F32),
            pltpu.VMEM((TS2, D_MODEL), F32),
            pltpu.VMEM((TS2, N_QK), BF16),
            pltpu.VMEM((TS2 // CHUNK, N_QK, GMLP_BLOCK), F32),
            pltpu.VMEM((GLA_HEADS, GLA_DK, GLA_DV), BF16),
        ],
        compiler_params=pltpu.CompilerParams(
            dimension_semantics=("arbitrary", "arbitrary"),
            vmem_limit_bytes=VMEM_LIMIT),
        name="mixer",
    )(qk, v, og, u, z, ga, gb, fg, wf2, bf2, gn, ws, bs_full)


def _outproj_kernel(m_ref, h0_ref, p_ref, wo_ref, wpg_ref, wpu_ref,
                    g1_ref, b1_ref, bpg_ref, h1_ref, ple_ref):
    def mix(c):
        rows = slice(c * RC3, (c + 1) * RC3)
        r = (ALPHA * h0_ref[rows, :].astype(F32)
             + jnp.dot(m_ref[rows, :], wo_ref[...], preferred_element_type=F32))
        hb = _ln_rows(r, g1_ref[...], b1_ref[...]).astype(BF16)
        h1_ref[rows, :] = hb
        return hb

    def embed(c, hb):
        rows = slice(c * RC3, (c + 1) * RC3)
        gate = _sigmoid(jnp.dot(hb, wpg_ref[...], preferred_element_type=F32) + bpg_ref[...])
        pu = jnp.dot(p_ref[rows, :].astype(BF16), wpu_ref[...], preferred_element_type=F32)
        ple_ref[rows, :] = (gate * pu).astype(BF16)

    nchunk = TM3 // RC3
    hb_next = mix(0)
    for c in range(nchunk):
        hb = hb_next
        if c + 1 < nchunk:
            hb_next = mix(c + 1)
        embed(c, hb)


def _outproj(m, h0, p2, wo, wpg, wpu, g1, b1, bpg):
    n = h0.shape[0]
    const = lambda shape: pl.BlockSpec(shape, lambda i: (0, 0), pipeline_mode=pl.Buffered(1))
    return pl.pallas_call(
        _outproj_kernel,
        grid=(n // TM3,),
        in_specs=[
            pl.BlockSpec((TM3, D_MODEL), lambda i: (i, 0)),
            pl.BlockSpec((TM3, D_MODEL), lambda i: (i, 0)),
            pl.BlockSpec((TM3, PLE_DIM), lambda i: (i, 0)),
            const((D_MODEL, D_MODEL)),
            const((D_MODEL, D_MODEL)),
            const((PLE_DIM, D_MODEL)),
            const((1, D_MODEL)), const((1, D_MODEL)), const((1, D_MODEL)),
        ],
        out_specs=[
            pl.BlockSpec((TM3, D_MODEL), lambda i: (i, 0)),
            pl.BlockSpec((TM3, D_MODEL), lambda i: (i, 0)),
        ],
        out_shape=[
            jax.ShapeDtypeStruct((n, D_MODEL), BF16),
            jax.ShapeDtypeStruct((n, D_MODEL), BF16),
        ],
        compiler_params=pltpu.CompilerParams(
            dimension_semantics=("arbitrary",),
            vmem_limit_bytes=VMEM_LIMIT),
        name="outproj",
    )(m, h0, p2, wo, wpg, wpu, g1, b1, bpg)


def _ffn_kernel(h1_ref, ple_ref, wg_ref, wu_ref, wd_ref, g2_ref, b2_ref, out_ref):
    f = pl.program_id(1)
    last = pl.num_programs(1) - 1

    def gate_up(r):
        hb = h1_ref[r * RC4:(r + 1) * RC4, :]
        gt = jnp.dot(hb, wg_ref[...], preferred_element_type=F32)
        up = jnp.dot(hb, wu_ref[...], preferred_element_type=F32)
        return (gt * _sigmoid(gt) * up).astype(BF16)

    def step(mode):
        nchunk = TM4 // RC4
        a_next = gate_up(0)
        for r in range(nchunk):
            rows = slice(r * RC4, (r + 1) * RC4)
            a = a_next
            if r + 1 < nchunk:
                a_next = gate_up(r + 1)
            d = jnp.dot(a, wd_ref[...], preferred_element_type=F32)
            if mode == "first":
                out_ref[rows, :] = d
            elif mode == "mid":
                out_ref[rows, :] += d
            else:
                res = (ALPHA * h1_ref[rows, :].astype(F32) + (out_ref[rows, :] + d)
                       + ple_ref[rows, :].astype(F32))
                out_ref[rows, :] = _ln_rows(res, g2_ref[...], b2_ref[...])

    pl.when(f == 0)(lambda: step("first"))
    pl.when((f > 0) & (f < last))(lambda: step("mid"))
    pl.when(f == last)(lambda: step("last"))


def _ffn(h1, ple, wgu, wd, g2, b2):
    n = h1.shape[0]
    nf = D_FF // TF4
    return pl.pallas_call(
        _ffn_kernel,
        grid=(n // TM4, nf),
        in_specs=[
            pl.BlockSpec((TM4, D_MODEL), lambda i, f: (i, 0)),
            pl.BlockSpec((TM4, D_MODEL), lambda i, f: (i, 0)),
            pl.BlockSpec((D_MODEL, TF4), lambda i, f: (0, f)),
            pl.BlockSpec((D_MODEL, TF4), lambda i, f: (0, f + nf)),
            pl.BlockSpec((TF4, D_MODEL), lambda i, f: (f, 0)),
            pl.BlockSpec((1, D_MODEL), lambda i, f: (0, 0)),
            pl.BlockSpec((1, D_MODEL), lambda i, f: (0, 0)),
        ],
        out_specs=pl.BlockSpec((TM4, D_MODEL), lambda i, f: (i, 0)),
        out_shape=jax.ShapeDtypeStruct((n, D_MODEL), F32),
        compiler_params=pltpu.CompilerParams(
            dimension_semantics=("arbitrary", "arbitrary"),
            vmem_limit_bytes=VMEM_LIMIT),
        name="ffn",
    )(h1, ple, wgu, wgu, wd, g2, b2)


def kernel(x, p, ln0_g, ln0_b, w_in, b_in, w_f2, b_f2, gla_norm_g, gmlp_ln_g, gmlp_ln_b,
           w_s, b_s, w_o, ln1_g, ln1_b, w_gu, w_down, w_pg, b_pg, w_pu, ln2_g, ln2_b):
    bsz, seq, d = x.shape
    assert d == D_MODEL and w_in.shape[0] == 1
    n = bsz * seq
    assert n % TM1 == 0 and seq % TS2 == 0 and n % TM4 == 0 and n % TM3 == 0 and n % TMQ == 0
    row = lambda a: a.reshape(1, -1).astype(F32)

    wi, bi = w_in[0], b_in[0]
    col_b = FG_COL0 + FG_WIDTH
    w_all = wi.astype(BF16)
    w_b = wi[:, col_b:].astype(BF16)
    b_all = bi.reshape(1, -1)
    b_b = bi[col_b:].reshape(1, -1)
    wf2 = jnp.zeros((FG_PAD, N_QK), F32)
    for h in range(GLA_HEADS):
        wf2 = wf2.at[h * GLA_RANK:(h + 1) * GLA_RANK, h * GLA_DK:(h + 1) * GLA_DK].set(w_f2[0, h])
    wf2 = wf2.astype(BF16)
    bs_full = jnp.repeat(b_s[0].T, GMLP_DG, axis=1)

    x2 = x.reshape(n, d)
    p2 = p[0].reshape(n, PLE_DIM)

    h0, qk, v, fg = _qkv(x2, row(ln0_g), row(ln0_b), w_all, b_all)
    zg, zb = row(gmlp_ln_g[0]), row(gmlp_ln_b[0])
    og, = _proj(h0, w_all, b_all, 2, 1, _act_silu, zg, zb, "proj_og")
    u, = _proj(h0, w_b, b_b, 0, 1, _act_gelu, zg, zb, "proj_u")
    z, = _proj(h0, w_b, b_b, 1, 1, _act_gelu_groupln, zg, zb, "proj_z")
    ga, gb = _proj(h0, w_b, b_b, 2, 2, _act_sigmoid, zg, zb, "proj_gates")
    m = _mixer(qk, v, og, u, z, ga, gb, fg, wf2, row(b_f2[0]), row(gla_norm_g[0]), w_s[0],
               bs_full, bsz, seq)
    h1, ple = _outproj(m, h0, p2, w_o[0].astype(BF16), w_pg[0].astype(BF16),
                       w_pu[0].astype(BF16), row(ln1_g[0]), row(ln1_b[0]), row(b_pg[0]))
    out = _ffn(h1, ple, w_gu[0].astype(BF16), w_down[0].astype(BF16),
               row(ln2_g[0]), row(ln2_b[0]))
    return out.reshape(bsz, seq, d)
```

```python
import functools

import jax
import jax.numpy as jnp
from jax import lax
from jax.experimental import pallas as pl
from jax.experimental.pallas import tpu as pltpu

F32 = jnp.float32
BF16 = jnp.bfloat16

D_MODEL = 2048
CHUNK = 64
GLA_HEADS = 4
GLA_DK = 256
GLA_DV = 512
GLA_RANK = 16
GLA_TAU = 16.0
GMLP_BLOCK = 128
GMLP_GROUPS = 4
GMLP_DG = 512
D_FF = 5632
PLE_DIM = 256
ALPHA = 2.0 ** 0.25
LN_EPS = 1e-5
SQRT_HALF = 0.7071067811865476

N_QK = GLA_HEADS * GLA_DK
FG_COL0 = 2 * N_QK + 2 * D_MODEL
FG_WIDTH = GLA_HEADS * GLA_RANK
FG_PAD = 128

VMEM_LIMIT = 56 * 1024 * 1024

TMQ, RCQ = 512, 256
TM1, RC1 = 1024, 256
TS2 = 512
HEAD_GROUP = 2
TM3, RC3 = 512, 256
TM4, TF4, RC4 = 1024, 512, 256


def _ln_rows(x, g, b):
    mu = jnp.mean(x, axis=-1, keepdims=True)
    xc = x - mu
    var = jnp.mean(xc * xc, axis=-1, keepdims=True)
    return xc * lax.rsqrt(var + LN_EPS) * g + b


def _gelu(x):
    return 0.5 * x * (1.0 + lax.erf(x * SQRT_HALF))


def _sigmoid(x):
    return 0.5 * jnp.tanh(0.5 * x) + 0.5


def _qkv_kernel(x_ref, g0_ref, b0_ref, w_ref, b_ref, wfg_ref, bfg_ref,
                h0_ref, qk_ref, v_ref, fg_ref):
    def ln(c):
        rows = slice(c * RCQ, (c + 1) * RCQ)
        hb = _ln_rows(x_ref[rows, :], g0_ref[...], b0_ref[...]).astype(BF16)
        h0_ref[rows, :] = hb
        return hb

    nchunk = TMQ // RCQ
    hb_next = ln(0)
    for c in range(nchunk):
        rows = slice(c * RCQ, (c + 1) * RCQ)
        hb = hb_next
        if c + 1 < nchunk:
            hb_next = ln(c + 1)
        acc = jnp.dot(hb, w_ref[...], preferred_element_type=F32) + b_ref[...]
        qk_ref[rows, :] = acc[:, :D_MODEL].astype(BF16)
        v_ref[rows, :] = acc[:, D_MODEL:].astype(BF16)
        fg = jnp.dot(hb, wfg_ref[...], preferred_element_type=F32) + bfg_ref[...]
        fg_ref[rows, :] = fg.astype(BF16)


def _qkv(x2, g0, b0, w, bias):
    n = x2.shape[0]
    fg_blk = FG_COL0 // FG_PAD
    const = lambda shape, idx: pl.BlockSpec(shape, lambda i: idx, pipeline_mode=pl.Buffered(1))
    tile = lambda width: pl.BlockSpec((TMQ, width), lambda i: (i, 0))
    return pl.pallas_call(
        _qkv_kernel,
        grid=(n // TMQ,),
        in_specs=[
            tile(D_MODEL),
            const((1, D_MODEL), (0, 0)),
            const((1, D_MODEL), (0, 0)),
            const((D_MODEL, 2 * D_MODEL), (0, 0)),
            const((1, 2 * D_MODEL), (0, 0)),
            const((D_MODEL, FG_PAD), (0, fg_blk)),
            const((1, FG_PAD), (0, fg_blk)),
        ],
        out_specs=[tile(D_MODEL), tile(D_MODEL), tile(D_MODEL), tile(FG_PAD)],
        out_shape=[
            jax.ShapeDtypeStruct((n, D_MODEL), BF16),
            jax.ShapeDtypeStruct((n, D_MODEL), BF16),
            jax.ShapeDtypeStruct((n, D_MODEL), BF16),
            jax.ShapeDtypeStruct((n, FG_PAD), BF16),
        ],
        compiler_params=pltpu.CompilerParams(
            dimension_semantics=("arbitrary",),
            vmem_limit_bytes=VMEM_LIMIT),
        name="ln0_qkv",
    )(x2, g0, b0, w, bias, w, bias)


def _proj_kernel(nsec, act, stage, hb_ref, w_ref, b_ref, zg_ref, zb_ref, *refs):
    out_refs = refs[:nsec]
    s = pl.program_id(0)

    def run(out_ref):
        for ci in range(TM1 // RC1):
            rows = slice(ci * RC1, (ci + 1) * RC1)
            acc = jnp.dot(hb_ref[rows, :], w_ref[...], preferred_element_type=F32) + b_ref[...]
            if stage:
                refs[nsec][ci % 2] = acc
                acc = refs[nsec][ci % 2]
            out_ref[rows, :] = act(acc, zg_ref, zb_ref).astype(BF16)

    if nsec == 1:
        run(out_refs[0])
    else:
        for k in range(nsec):
            pl.when(s == k)(functools.partial(run, out_refs[k]))


def _act_silu(a, zg_ref, zb_ref):
    return a * _sigmoid(a)


def _act_gelu(a, zg_ref, zb_ref):
    return _gelu(a)


def _act_sigmoid(a, zg_ref, zb_ref):
    return _sigmoid(a)


def _act_gelu_groupln(a, zg_ref, zb_ref):
    parts = []
    for g in range(GMLP_GROUPS):
        cs = slice(g * GMLP_DG, (g + 1) * GMLP_DG)
        parts.append(_ln_rows(_gelu(a[:, cs]), zg_ref[:, cs], zb_ref[:, cs]))
    return jnp.concatenate(parts, axis=-1)


def _proj(hb, w, bias, blk0, nsec, act, zg, zb, name, stage=False):
    n = hb.shape[0]
    nt = n // TM1

    def out_idx(k):
        return lambda s, i: (jnp.where(s == k, i, jnp.where(s > k, nt - 1, 0)), 0)

    in_specs = [
        pl.BlockSpec((TM1, D_MODEL), lambda s, i: (i, 0)),
        pl.BlockSpec((D_MODEL, D_MODEL), lambda s, i: (0, blk0 + s)),
        pl.BlockSpec((1, D_MODEL), lambda s, i: (0, blk0 + s)),
        pl.BlockSpec((1, D_MODEL), lambda s, i: (0, 0)),
        pl.BlockSpec((1, D_MODEL), lambda s, i: (0, 0)),
    ]
    operands = [hb, w, bias, zg, zb]
    out_specs = [pl.BlockSpec((TM1, D_MODEL), out_idx(k)) for k in range(nsec)]
    out_shape = [jax.ShapeDtypeStruct((n, D_MODEL), BF16) for _ in range(nsec)]
    return pl.pallas_call(
        functools.partial(_proj_kernel, nsec, act, stage),
        grid=(nsec, nt),
        in_specs=in_specs,
        out_specs=out_specs,
        out_shape=out_shape,
        scratch_shapes=[pltpu.VMEM((2, RC1, D_MODEL), F32)] if stage else [],
        compiler_params=pltpu.CompilerParams(
            dimension_semantics=("arbitrary", "arbitrary"),
            vmem_limit_bytes=VMEM_LIMIT),
        name=name,
    )(*operands)


def _mixer_kernel(qk_ref, v_ref, og_ref, u_ref, z_ref, ga_ref, gb_ref,
                  fg_ref, wf2_ref, bf2_ref, gn_ref, ws_ref, bs_ref,
                  m_ref, st_ref, yb_ref, kt_ref, dec_ref, stb_ref):
    @pl.when(pl.program_id(1) == 0)
    def _():
        st_ref[...] = jnp.zeros_like(st_ref)

    ri = lax.broadcasted_iota(jnp.int32, (GMLP_BLOCK, GMLP_BLOCK), 0) // CHUNK
    ci = lax.broadcasted_iota(jnp.int32, (GMLP_BLOCK, GMLP_BLOCK), 1) // CHUNK
    for g in range(GMLP_GROUPS):
        cs = slice(g * GMLP_DG, (g + 1) * GMLP_DG)
        wm = jnp.where(ri >= ci, ws_ref[g], 0.0).astype(BF16)
        for nb in range(TS2 // GMLP_BLOCK):
            rows = slice(nb * GMLP_BLOCK, (nb + 1) * GMLP_BLOCK)
            sg = jnp.dot(wm, z_ref[rows, cs], preferred_element_type=F32) + bs_ref[:, cs]
            yb_ref[rows, cs] = u_ref[rows, cs].astype(F32) * sg

    fl = jnp.dot(fg_ref[...], wf2_ref[...], preferred_element_type=F32) + bf2_ref[...]
    la = (jnp.minimum(fl, 0.0) - jnp.log1p(jnp.exp(-jnp.abs(fl)))) * (1.0 / GLA_TAU)

    tri = (lax.broadcasted_iota(jnp.int32, (CHUNK, CHUNK), 0)
           >= lax.broadcasted_iota(jnp.int32, (CHUNK, CHUNK), 1)).astype(BF16)

    heads = range(GLA_HEADS)
    kcs = [slice(h * GLA_DK, (h + 1) * GLA_DK) for h in heads]
    vcs = [slice(h * GLA_DV, (h + 1) * GLA_DV) for h in heads]

    for c in range(TS2 // CHUNK):
        rows = slice(c * CHUNK, (c + 1) * CHUNK)
        la_c = la[rows, :]
        p0 = la_c.astype(BF16)
        r0 = la_c - p0.astype(F32)
        p1 = r0.astype(BF16)
        p2 = (r0 - p1.astype(F32)).astype(BF16)
        b = (jnp.dot(tri, p0, preferred_element_type=F32)
             + jnp.dot(tri, p1, preferred_element_type=F32)
             + jnp.dot(tri, p2, preferred_element_type=F32))
        bend = b[CHUNK - 1:CHUNK, :]
        kt_ref[rows, :] = (qk_ref[rows, N_QK:].astype(F32) * jnp.exp(bend - b)).astype(BF16)
        dec_ref[c] = jnp.transpose(jnp.broadcast_to(jnp.exp(bend), (GMLP_BLOCK, N_QK)))

    for c in range(TS2 // CHUNK):
        rows = slice(c * CHUNK, (c + 1) * CHUNK)
        for h0 in range(0, GLA_HEADS, HEAD_GROUP):
            grp = range(h0, h0 + HEAD_GROUP)
            upd = {h: lax.dot_general(kt_ref[rows, kcs[h]], v_ref[rows, vcs[h]],
                                      (((0,), (0,)), ((), ())), preferred_element_type=F32)
                   for h in grp}
            for h in grp:
                dcol = jnp.concatenate([dec_ref[c, kcs[h], :]] * (GLA_DV // 128), axis=1)
                st = st_ref[h] * dcol + upd[h]
                st_ref[h] = st
                stb_ref[h] = st.astype(BF16)
            o = {h: jnp.dot(qk_ref[rows, kcs[h]], stb_ref[h], preferred_element_type=F32)
                 * (GLA_DK ** -0.5) for h in grp}
            for h in grp:
                vc = vcs[h]
                ms = jnp.mean(o[h] * o[h], axis=-1, keepdims=True)
                ya = (o[h] * lax.rsqrt(ms + LN_EPS) * gn_ref[:, vc]) * og_ref[rows, vc].astype(F32)
                mm = (ga_ref[rows, vc].astype(F32) * ya
                      + gb_ref[rows, vc].astype(F32) * yb_ref[rows, vc])
                m_ref[rows, vc] = mm.astype(BF16)


def _mixer(qk, v, og, u, z, ga, gb, fg, wf2, bf2, gn, ws, bs_full, bsz, seq):
    nt = seq // TS2
    tile = lambda b, s: (b * nt + s, 0)
    return pl.pallas_call(
        _mixer_kernel,
        grid=(bsz, nt),
        in_specs=[
            pl.BlockSpec((TS2, D_MODEL), tile),
            pl.BlockSpec((TS2, D_MODEL), tile),
            pl.BlockSpec((TS2, D_MODEL), tile),
            pl.BlockSpec((TS2, D_MODEL), tile),
            pl.BlockSpec((TS2, D_MODEL), tile),
            pl.BlockSpec((TS2, D_MODEL), tile),
            pl.BlockSpec((TS2, D_MODEL), tile),
            pl.BlockSpec((TS2, FG_PAD), tile),
            pl.BlockSpec((FG_PAD, N_QK), lambda b, s: (0, 0)),
            pl.BlockSpec((1, N_QK), lambda b, s: (0, 0)),
            pl.BlockSpec((1, D_MODEL), lambda b, s: (0, 0)),
            pl.BlockSpec((GMLP_GROUPS, GMLP_BLOCK, GMLP_BLOCK), lambda b, s: (0, 0, 0)),
            pl.BlockSpec((GMLP_BLOCK, D_MODEL), lambda b, s: (0, 0)),
        ],
        out_specs=pl.BlockSpec((TS2, D_MODEL), tile),
        out_shape=jax.ShapeDtypeStruct((bsz * seq, D_MODEL), BF16),
        scratch_shapes=[
            pltpu.VMEM((GLA_HEADS, GLA_DK, GLA_DV), F32),
            pltpu.VMEM((TS2, D_MODEL), F32),
            pltpu.VMEM((TS2, N_QK), BF16),
            pltpu.VMEM((TS2 // CHUNK, N_QK, GMLP_BLOCK), F32),
            pltpu.VMEM((GLA_HEADS, GLA_DK, GLA_DV), BF16),
        ],
        compiler_params=pltpu.CompilerParams(
            dimension_semantics=("arbitrary", "arbitrary"),
            vmem_limit_bytes=VMEM_LIMIT),
        name="mixer",
    )(qk, v, og, u, z, ga, gb, fg, wf2, bf2, gn, ws, bs_full)


def _outproj_kernel(m_ref, h0_ref, p_ref, wo_ref, wpg_ref, wpu_ref,
                    g1_ref, b1_ref, bpg_ref, h1_ref, ple_ref):
    def mix(c):
        rows = slice(c * RC3, (c + 1) * RC3)
        r = (ALPHA * h0_ref[rows, :].astype(F32)
             + jnp.dot(m_ref[rows, :], wo_ref[...], preferred_element_type=F32))
        hb = _ln_rows(r, g1_ref[...], b1_ref[...]).astype(BF16)
        h1_ref[rows, :] = hb
        return hb

    def embed(c, hb):
        rows = slice(c * RC3, (c + 1) * RC3)
        gate = _sigmoid(jnp.dot(hb, wpg_ref[...], preferred_element_type=F32) + bpg_ref[...])
        pu = jnp.dot(p_ref[rows, :].astype(BF16), wpu_ref[...], preferred_element_type=F32)
        ple_ref[rows, :] = (gate * pu).astype(BF16)

    nchunk = TM3 // RC3
    hb_next = mix(0)
    for c in range(nchunk):
        hb = hb_next
        if c + 1 < nchunk:
            hb_next = mix(c + 1)
        embed(c, hb)


def _outproj(m, h0, p2, wo, wpg, wpu, g1, b1, bpg):
    n = h0.shape[0]
    const = lambda shape: pl.BlockSpec(shape, lambda i: (0, 0), pipeline_mode=pl.Buffered(1))
    return pl.pallas_call(
        _outproj_kernel,
        grid=(n // TM3,),
        in_specs=[
            pl.BlockSpec((TM3, D_MODEL), lambda i: (i, 0)),
            pl.BlockSpec((TM3, D_MODEL), lambda i: (i, 0)),
            pl.BlockSpec((TM3, PLE_DIM), lambda i: (i, 0)),
            const((D_MODEL, D_MODEL)),
            const((D_MODEL, D_MODEL)),
            const((PLE_DIM, D_MODEL)),
            const((1, D_MODEL)), const((1, D_MODEL)), const((1, D_MODEL)),
        ],
        out_specs=[
            pl.BlockSpec((TM3, D_MODEL), lambda i: (i, 0)),
            pl.BlockSpec((TM3, D_MODEL), lambda i: (i, 0)),
        ],
        out_shape=[
            jax.ShapeDtypeStruct((n, D_MODEL), BF16),
            jax.ShapeDtypeStruct((n, D_MODEL), BF16),
        ],
        compiler_params=pltpu.CompilerParams(
            dimension_semantics=("arbitrary",),
            vmem_limit_bytes=VMEM_LIMIT),
        name="outproj",
    )(m, h0, p2, wo, wpg, wpu, g1, b1, bpg)


def _ffn_kernel(h1_ref, ple_ref, wg_ref, wu_ref, wd_ref, g2_ref, b2_ref, out_ref):
    f = pl.program_id(1)
    last = pl.num_programs(1) - 1

    def gate_up(r):
        hb = h1_ref[r * RC4:(r + 1) * RC4, :]
        gt = jnp.dot(hb, wg_ref[...], preferred_element_type=F32)
        up = jnp.dot(hb, wu_ref[...], preferred_element_type=F32)
        return (gt * _sigmoid(gt) * up).astype(BF16)

    def step(mode):
        nchunk = TM4 // RC4
        a_next = gate_up(0)
        for r in range(nchunk):
            rows = slice(r * RC4, (r + 1) * RC4)
            a = a_next
            if r + 1 < nchunk:
                a_next = gate_up(r + 1)
            d = jnp.dot(a, wd_ref[...], preferred_element_type=F32)
            if mode == "first":
                out_ref[rows, :] = d
            elif mode == "mid":
                out_ref[rows, :] += d
            else:
                res = (ALPHA * h1_ref[rows, :].astype(F32) + (out_ref[rows, :] + d)
                       + ple_ref[rows, :].astype(F32))
                out_ref[rows, :] = _ln_rows(res, g2_ref[...], b2_ref[...])

    pl.when(f == 0)(lambda: step("first"))
    pl.when((f > 0) & (f < last))(lambda: step("mid"))
    pl.when(f == last)(lambda: step("last"))


def _ffn(h1, ple, wgu, wd, g2, b2):
    n = h1.shape[0]
    nf = D_FF // TF4
    return pl.pallas_call(
        _ffn_kernel,
        grid=(n // TM4, nf),
        in_specs=[
            pl.BlockSpec((TM4, D_MODEL), lambda i, f: (i, 0)),
            pl.BlockSpec((TM4, D_MODEL), lambda i, f: (i, 0)),
            pl.BlockSpec((D_MODEL, TF4), lambda i, f: (0, f)),
            pl.BlockSpec((D_MODEL, TF4), lambda i, f: (0, f + nf)),
            pl.BlockSpec((TF4, D_MODEL), lambda i, f: (f, 0)),
            pl.BlockSpec((1, D_MODEL), lambda i, f: (0, 0)),
            pl.BlockSpec((1, D_MODEL), lambda i, f: (0, 0)),
        ],
        out_specs=pl.BlockSpec((TM4, D_MODEL), lambda i, f: (i, 0)),
        out_shape=jax.ShapeDtypeStruct((n, D_MODEL), F32),
        compiler_params=pltpu.CompilerParams(
            dimension_semantics=("arbitrary", "arbitrary"),
            vmem_limit_bytes=VMEM_LIMIT),
        name="ffn",
    )(h1, ple, wgu, wgu, wd, g2, b2)


def kernel(x, p, ln0_g, ln0_b, w_in, b_in, w_f2, b_f2, gla_norm_g, gmlp_ln_g, gmlp_ln_b,
           w_s, b_s, w_o, ln1_g, ln1_b, w_gu, w_down, w_pg, b_pg, w_pu, ln2_g, ln2_b):
    bsz, seq, d = x.shape
    assert d == D_MODEL and w_in.shape[0] == 1
    n = bsz * seq
    assert n % TM1 == 0 and seq % TS2 == 0 and n % TM4 == 0 and n % TM3 == 0 and n % TMQ == 0
    row = lambda a: a.reshape(1, -1).astype(F32)

    wi, bi = w_in[0], b_in[0]
    col_b = FG_COL0 + FG_WIDTH
    w_all = wi.astype(BF16)
    w_b = wi[:, col_b:].astype(BF16)
    b_all = bi.reshape(1, -1)
    b_b = bi[col_b:].reshape(1, -1)
    wf2 = jnp.zeros((FG_PAD, N_QK), F32)
    for h in range(GLA_HEADS):
        wf2 = wf2.at[h * GLA_RANK:(h + 1) * GLA_RANK, h * GLA_DK:(h + 1) * GLA_DK].set(w_f2[0, h])
    wf2 = wf2.astype(BF16)
    bs_full = jnp.repeat(b_s[0].T, GMLP_DG, axis=1)

    x2 = x.reshape(n, d)
    p2 = p[0].reshape(n, PLE_DIM)

    h0, qk, v, fg = _qkv(x2, row(ln0_g), row(ln0_b), w_all, b_all)
    zg, zb = row(gmlp_ln_g[0]), row(gmlp_ln_b[0])
    og, = _proj(h0, w_all, b_all, 2, 1, _act_silu, zg, zb, "proj_og")
    u, = _proj(h0, w_b, b_b, 0, 1, _act_gelu, zg, zb, "proj_u")
    z, = _proj(h0, w_b, b_b, 1, 1, _act_gelu_groupln, zg, zb, "proj_z")
    ga, gb = _proj(h0, w_b, b_b, 2, 2, _act_sigmoid, zg, zb, "proj_gates", stage=True)
    m = _mixer(qk, v, og, u, z, ga, gb, fg, wf2, row(b_f2[0]), row(gla_norm_g[0]), w_s[0],
               bs_full, bsz, seq)
    h1, ple = _outproj(m, h0, p2, w_o[0].astype(BF16), w_pg[0].astype(BF16),
                       w_pu[0].astype(BF16), row(ln1_g[0]), row(ln1_b[0]), row(b_pg[0]))
    out = _ffn(h1, ple, w_gu[0].astype(BF16), w_down[0].astype(BF16),
               row(ln2_g[0]), row(ln2_b[0]))
    return out.reshape(bsz, seq, d)
```
